```python
import math
import jax, jax.numpy as jnp
from jax import lax
import numpy as np

D_MODEL = 1024
BATCH = 4
SEQ = 8192
DEPTH = 2

CHUNK = 64
MIX_WIDTH = D_MODEL
GMLP_WIDTH = MIX_WIDTH // 2
GMLP_GROUPS = 4
GMLP_GROUP_DIM = GMLP_WIDTH // GMLP_GROUPS
GMLP_BLOCK = 128
DIFF_WIDTH = MIX_WIDTH - GMLP_WIDTH
DIFF_HEADS = 4
DIFF_VDIM = DIFF_WIDTH // DIFF_HEADS
DIFF_QK_DIM = DIFF_VDIM // 2
ROPE_THETA = 10000.0
QBLK = 128
FFN_HIDDEN = int(math.ceil(8 * D_MODEL / 3 / 256)) * 256
Q_WIDTH = DIFF_HEADS * 2 * DIFF_QK_DIM
K_WIDTH = DIFF_HEADS * 2 * DIFF_QK_DIM
IN_WIDTH = 2 * GMLP_WIDTH + Q_WIDTH + K_WIDTH + DIFF_WIDTH
NORM_EPS = 1e-6
NEG_INF = -1e30

kernel_name = "hybrid_gmlp_diffattn_sandwich"


def rms_norm(x, g, eps=NORM_EPS):
    xf = x.astype(jnp.float32)
    y = xf * lax.rsqrt(jnp.mean(xf * xf, axis=-1, keepdims=True) + eps)
    return (y * g.astype(jnp.float32)).astype(x.dtype)


def layer_norm(x, g, b, eps=NORM_EPS):
    xf = x.astype(jnp.float32)
    mu = jnp.mean(xf, axis=-1, keepdims=True)
    xc = xf - mu
    y = xc * lax.rsqrt(jnp.mean(xc * xc, axis=-1, keepdims=True) + eps)
    return (y * g.astype(jnp.float32) + b.astype(jnp.float32)).astype(x.dtype)


def rope_tables(seq, dim):
    inv = 1.0 / (ROPE_THETA ** (jnp.arange(0, dim, 2, dtype=jnp.float32) / dim))
    ang = jnp.arange(seq, dtype=jnp.float32)[:, None] * inv[None, :]
    return jnp.cos(ang), jnp.sin(ang)


def apply_rope(x, cos, sin):
    xf = x.astype(jnp.float32)
    half = xf.shape[-1] // 2
    x1, x2 = xf[..., :half], xf[..., half:]
    c = cos[:, None, None, :]
    s = sin[:, None, None, :]
    return jnp.concatenate([x1 * c - x2 * s, x2 * c + x1 * s], axis=-1).astype(x.dtype)


def gmlp_mixer(u, v, ln_g, ln_b, ws, bias):
    bsz, seq, _ = u.shape
    n = seq // GMLP_BLOCK
    v = layer_norm(v, ln_g, ln_b)
    v = v.reshape(bsz, n, GMLP_BLOCK, GMLP_GROUPS, GMLP_GROUP_DIM)
    t_chunk = jnp.arange(GMLP_BLOCK) // CHUNK
    mask = t_chunk[None, :] <= t_chunk[:, None]
    w = jnp.where(mask[None], ws, jnp.zeros_like(ws)).astype(v.dtype)
    mixed = jnp.einsum('gts,bnsgc->bntgc', w, v)
    mixed = mixed + jnp.transpose(bias)[:, :, None].astype(v.dtype)
    out = u.reshape(bsz, n, GMLP_BLOCK, GMLP_GROUPS, GMLP_GROUP_DIM) * mixed
    return out.reshape(bsz, seq, GMLP_WIDTH)


def diff_attention(q, k, v, cos, sin, lam, lam_init, subln_g):
    bsz, seq, _ = q.shape
    q = q.reshape(bsz, seq, DIFF_HEADS, 2, DIFF_QK_DIM)
    k = k.reshape(bsz, seq, DIFF_HEADS, 2, DIFF_QK_DIM)
    v = v.reshape(bsz, seq, DIFF_HEADS, DIFF_VDIM)
    q = apply_rope(q, cos, sin) * (DIFF_QK_DIM ** -0.5)
    k = apply_rope(k, cos, sin)
    k_chunk = jnp.arange(seq) // CHUNK
    n_blk = seq // QBLK

    def block(i):
        qb = lax.dynamic_slice_in_dim(q, i * QBLK, QBLK, axis=1)
        s = jnp.einsum('bqhjd,bkhjd->jbhqk', qb, k).astype(jnp.float32)
        q_chunk = (i * QBLK + jnp.arange(QBLK)) // CHUNK
        mask = k_chunk[None, :] <= q_chunk[:, None]
        s = jnp.where(mask, s, NEG_INF)
        p = jax.nn.softmax(s, axis=-1)
        a = p[0] - lam * p[1]
        return jnp.einsum('bhqk,bkhe->bqhe', a.astype(v.dtype), v)

    o = lax.map(block, jnp.arange(n_blk))
    o = jnp.transpose(o, (1, 0, 2, 3, 4)).reshape(bsz, seq, DIFF_HEADS, DIFF_VDIM)
    o = rms_norm(o, subln_g) * (1.0 - lam_init)
    return o.reshape(bsz, seq, DIFF_WIDTH).astype(q.dtype)


def setup_inputs(seed: int = 0) -> dict:
    key = jax.random.key(seed)
    ks = jax.random.split(key, 24)
    f32 = jnp.float32
    nrm = lambda k, shape, scale: jax.random.normal(k, shape, f32) * scale
    gain = lambda k, shape: 1.0 + 0.05 * jax.random.normal(k, shape, f32)
    return {
        "x": jax.random.normal(ks[0], (BATCH, SEQ, D_MODEL), f32),
        "pre_mix_g": gain(ks[1], (DEPTH, D_MODEL)),
        "w_in": nrm(ks[2], (DEPTH, D_MODEL, IN_WIDTH), D_MODEL ** -0.5),
        "gmlp_ln_g": gain(ks[3], (DEPTH, GMLP_WIDTH)),
        "gmlp_ln_b": nrm(ks[4], (DEPTH, GMLP_WIDTH), 0.02),
        "gmlp_ws": nrm(ks[5], (DEPTH, GMLP_GROUPS, GMLP_BLOCK, GMLP_BLOCK), GMLP_BLOCK ** -0.5),
        "gmlp_b": gain(ks[6], (DEPTH, GMLP_GROUPS, GMLP_BLOCK)),
        "lambda_q1": nrm(ks[7], (DEPTH, DIFF_QK_DIM), 0.1),
        "lambda_k1": nrm(ks[8], (DEPTH, DIFF_QK_DIM), 0.1),
        "lambda_q2": nrm(ks[9], (DEPTH, DIFF_QK_DIM), 0.1),
        "lambda_k2": nrm(ks[10], (DEPTH, DIFF_QK_DIM), 0.1),
        "subln_g": gain(ks[11], (DEPTH, DIFF_VDIM)),
        "w_out": nrm(ks[12], (DEPTH, MIX_WIDTH, D_MODEL), MIX_WIDTH ** -0.5),
        "post_mix_g": gain(ks[13], (DEPTH, D_MODEL)),
        "pre_ffn_g": gain(ks[14], (DEPTH, D_MODEL)),
        "w_gate_up": nrm(ks[15], (DEPTH, D_MODEL, 2 * FFN_HIDDEN), D_MODEL ** -0.5),
        "w_down": nrm(ks[16], (DEPTH, FFN_HIDDEN, D_MODEL), FFN_HIDDEN ** -0.5),
        "post_ffn_g": gain(ks[17], (DEPTH, D_MODEL)),
    }


def reference(x, pre_mix_g, w_in, gmlp_ln_g, gmlp_ln_b, gmlp_ws, gmlp_b,
              lambda_q1, lambda_k1, lambda_q2, lambda_k2, subln_g, w_out,
              post_mix_g, pre_ffn_g, w_gate_up, w_down, post_ffn_g):
    seq = x.shape[1]
    cos, sin = rope_tables(seq, DIFF_QK_DIM)
    o_u = 0
    o_v = GMLP_WIDTH
    o_q = 2 * GMLP_WIDTH
    o_k = o_q + Q_WIDTH
    o_val = o_k + K_WIDTH
    for l in range(DEPTH):
        h = rms_norm(x, pre_mix_g[l])
        z = jnp.einsum('bsd,de->bse', h, w_in[l])
        uv = jax.nn.gelu(z[..., o_u:o_q], approximate=False)
        u, v_g = uv[..., :GMLP_WIDTH], uv[..., GMLP_WIDTH:]
        y_a = gmlp_mixer(u, v_g, gmlp_ln_g[l], gmlp_ln_b[l], gmlp_ws[l], gmlp_b[l])

        lam_init = 0.8 - 0.6 * math.exp(-0.3 * l)
        lam = (jnp.exp(jnp.sum(lambda_q1[l].astype(jnp.float32) * lambda_k1[l].astype(jnp.float32)))
               - jnp.exp(jnp.sum(lambda_q2[l].astype(jnp.float32) * lambda_k2[l].astype(jnp.float32)))
               + lam_init)
        y_b = diff_attention(z[..., o_q:o_k], z[..., o_k:o_val], z[..., o_val:],
                             cos, sin, lam, lam_init, subln_g[l])

        mix = jnp.einsum('bse,ed->bsd', jnp.concatenate([y_a, y_b], axis=-1), w_out[l])
        x = x + rms_norm(mix, post_mix_g[l])

        h = rms_norm(x, pre_ffn_g[l])
        gu = jnp.einsum('bsd,df->bsf', h, w_gate_up[l])
        gate, up = gu[..., :FFN_HIDDEN], gu[..., FFN_HIDDEN:]
        y = jnp.einsum('bsf,fd->bsd', jax.nn.silu(gate) * up, w_down[l])
        x = x + rms_norm(y, post_ffn_g[l])
    return x
```

```python
import functools
import math

import jax
import jax.numpy as jnp
from jax import lax
from jax.experimental import pallas as pl
from jax.experimental.pallas import tpu as pltpu

D_MODEL = 1024
CHUNK = 64
GMLP_WIDTH = 512
GMLP_GROUPS = 4
GMLP_GROUP_DIM = 128
GMLP_BLOCK = 128
DIFF_WIDTH = 512
DIFF_HEADS = 4
DIFF_VDIM = 128
DIFF_QK_DIM = 64
ROPE_THETA = 10000.0
FFN_HIDDEN = 2816
IN_WIDTH = 2560
NORM_EPS = 1e-6
NEG_INF = -1e30

LANES = 128
VMEM_LIMIT_BYTES = 56 * 1024 * 1024

ROW_TILE = 512
ATTN_TILE = 512
FFN_CHUNK = 256

_BF16 = jnp.bfloat16
_F32 = jnp.float32


def _dot(a, b):
    return jnp.dot(a, b, preferred_element_type=_F32)


def _rms(x, g):
    return x * lax.rsqrt(jnp.mean(x * x, axis=-1, keepdims=True) + NORM_EPS) * g


def _gelu(x):
    return 0.5 * x * (1.0 + lax.erf(x * math.sqrt(0.5)))


def _inproj_kernel(x_ref, g_ref, w_ref, lng_ref, lnb_ref, ws_ref, bias_ref,
                   cos_ref, sa_ref, sb_ref, ya_ref, q_ref, k_ref, v_ref):
    tm = x_ref.shape[0]
    h = _rms(x_ref[...], g_ref[...]).astype(_BF16)

    u = _gelu(_dot(h, w_ref[:, 0:GMLP_WIDTH]))
    vg = _gelu(_dot(h, w_ref[:, GMLP_WIDTH:2 * GMLP_WIDTH]))
    mu = jnp.mean(vg, axis=-1, keepdims=True)
    vc = vg - mu
    var = jnp.mean(vc * vc, axis=-1, keepdims=True)
    vln = (vc * lax.rsqrt(var + NORM_EPS) * lng_ref[...] + lnb_ref[...]).astype(_BF16)

    t_chunk = lax.broadcasted_iota(jnp.int32, (GMLP_BLOCK, GMLP_BLOCK), 0) // CHUNK
    s_chunk = lax.broadcasted_iota(jnp.int32, (GMLP_BLOCK, GMLP_BLOCK), 1) // CHUNK
    causal = s_chunk <= t_chunk
    for g in range(GMLP_GROUPS):
        w_g = jnp.where(causal, ws_ref[g], 0.0).astype(_BF16)
        cols = slice(g * GMLP_GROUP_DIM, (g + 1) * GMLP_GROUP_DIM)
        for r in range(tm // GMLP_BLOCK):
            rows = slice(r * GMLP_BLOCK, (r + 1) * GMLP_BLOCK)
            mixed = _dot(w_g, vln[rows, cols]) + bias_ref[g]
            ya_ref[rows, cols] = (u[rows, cols] * mixed).astype(_BF16)

    cos = cos_ref[...]
    sa = sa_ref[...]
    sb = sb_ref[...]

    def rope(z, scale):
        out = []
        for hb in range(z.shape[1] // LANES):
            blk = z[:, hb * LANES:(hb + 1) * LANES]
            rot = (blk * cos + pltpu.roll(blk, LANES - DIFF_QK_DIM // 2, 1) * sa
                   + pltpu.roll(blk, DIFF_QK_DIM // 2, 1) * sb)
            out.append((rot * scale).astype(_BF16))
        return out

    o_q = 2 * GMLP_WIDTH
    o_k = o_q + DIFF_WIDTH
    o_v = o_k + DIFF_WIDTH
    for hb, blk in enumerate(rope(_dot(h, w_ref[:, o_q:o_k]), DIFF_QK_DIM ** -0.5)):
        q_ref[:, hb * LANES:(hb + 1) * LANES] = blk
    for hb, blk in enumerate(rope(_dot(h, w_ref[:, o_k:o_v]), 1.0)):
        k_ref[:, hb * LANES:(hb + 1) * LANES] = blk
    v_ref[...] = _dot(h, w_ref[:, o_v:IN_WIDTH]).astype(_BF16)


def _inproj(x, g, w, lng, lnb, ws, bias, cos, sa, sb, seq):
    n, d = x.shape
    tm = ROW_TILE
    seq_tiles = seq // tm
    row = lambda i: (i, 0)
    fixed2 = lambda i: (0, 0)
    fixed3 = lambda i: (0, 0, 0)
    pos = lambda i: (i % seq_tiles, 0)
    out_sds = jax.ShapeDtypeStruct((n, DIFF_WIDTH), _BF16)
    return pl.pallas_call(
        _inproj_kernel,
        grid=(n // tm,),
        in_specs=[
            pl.BlockSpec((tm, d), row),
            pl.BlockSpec((1, d), fixed2),
            pl.BlockSpec((d, IN_WIDTH), fixed2),
            pl.BlockSpec((1, GMLP_WIDTH), fixed2),
            pl.BlockSpec((1, GMLP_WIDTH), fixed2),
            pl.BlockSpec((GMLP_GROUPS, GMLP_BLOCK, GMLP_BLOCK), fixed3),
            pl.BlockSpec((GMLP_GROUPS, GMLP_BLOCK, GMLP_GROUP_DIM), fixed3),
            pl.BlockSpec((tm, LANES), pos),
            pl.BlockSpec((tm, LANES), pos),
            pl.BlockSpec((tm, LANES), pos),
        ],
        out_specs=[pl.BlockSpec((tm, DIFF_WIDTH), row)] * 4,
        out_shape=[out_sds] * 4,
        compiler_params=pltpu.CompilerParams(
            dimension_semantics=("arbitrary",), vmem_limit_bytes=VMEM_LIMIT_BYTES),
        name="inproj",
    )(x, g, w, lng, lnb, ws, bias, cos, sa, sb)


def _attn_kernel(lam_ref, q_ref, k_ref, v_ref, subg_ref, o_ref,
                 qq_sc, m_sc, l_sc, acc_sc, *, lam_init):
    t = q_ref.shape[0]
    i = pl.program_id(2)

    q = q_ref[...]
    lane = lax.broadcasted_iota(jnp.int32, q.shape, 1)
    zero = jnp.zeros_like(q)
    qq_sc[0:t, :] = jnp.where(lane < DIFF_QK_DIM, q, zero)
    qq_sc[t:2 * t, :] = jnp.where(lane >= DIFF_QK_DIM, q, zero)
    m_sc[...] = jnp.full(m_sc.shape, NEG_INF, _F32)
    l_sc[...] = jnp.zeros(l_sc.shape, _F32)
    acc_sc[...] = jnp.zeros(acc_sc.shape, _F32)

    def step(j, masked):
        start = pl.multiple_of(j * t, t)
        kj = k_ref[pl.ds(start, t), :]
        vj = v_ref[pl.ds(start, t), :]
        s = lax.dot_general(qq_sc[...], kj, (((1,), (1,)), ((), ())),
                            preferred_element_type=_F32)
        if masked:
            qc = lax.broadcasted_iota(jnp.int32, s.shape, 0) % t // CHUNK
            kc = lax.broadcasted_iota(jnp.int32, s.shape, 1) // CHUNK
            s = jnp.where(kc <= qc, s, NEG_INF)
        m_prev = m_sc[...]
        m_new = jnp.maximum(m_prev, jnp.max(s, axis=1, keepdims=True))
        alpha = jnp.exp(m_prev - m_new)
        p = jnp.exp(s - m_new)
        l_sc[...] = alpha * l_sc[...] + jnp.sum(p, axis=1, keepdims=True)
        acc_sc[...] = alpha * acc_sc[...] + _dot(p.astype(_BF16), vj)
        m_sc[...] = m_new

    def body(j, carry):
        step(j, masked=False)
        return carry

    lax.fori_loop(0, i, body, 0)
    step(i, masked=True)

    lam_v = lam_ref[...]
    lam = (jnp.exp(jnp.sum(lam_v[0:1] * lam_v[1:2], axis=-1, keepdims=True))
           - jnp.exp(jnp.sum(lam_v[2:3] * lam_v[3:4], axis=-1, keepdims=True)) + lam_init)
    o = acc_sc[0:t, :] / l_sc[0:t, :] - lam * (acc_sc[t:2 * t, :] / l_sc[t:2 * t, :])
    o = _rms(o, subg_ref[...]) * (1.0 - lam_init)
    o_ref[...] = o.astype(_BF16)


def _attn(lam_vecs, q, k, v, subg, batch, seq, lam_init):
    t = ATTN_TILE
    nq = seq // t
    return pl.pallas_call(
        functools.partial(_attn_kernel, lam_init=lam_init),
        grid=(batch, DIFF_HEADS, nq),
        in_specs=[
            pl.BlockSpec((4, DIFF_QK_DIM), lambda b, h, i: (0, 0)),
            pl.BlockSpec((t, LANES), lambda b, h, i: (b * nq + i, h)),
            pl.BlockSpec((seq, LANES), lambda b, h, i: (b, h)),
            pl.BlockSpec((seq, LANES), lambda b, h, i: (b, h)),
            pl.BlockSpec((1, DIFF_VDIM), lambda b, h, i: (0, 0)),
        ],
        out_specs=pl.BlockSpec((t, LANES), lambda b, h, i: (b * nq + i, h)),
        out_shape=jax.ShapeDtypeStruct(q.shape, _BF16),
        scratch_shapes=[
            pltpu.VMEM((2 * t, LANES), _BF16),
            pltpu.VMEM((2 * t, 1), _F32),
            pltpu.VMEM((2 * t, 1), _F32),
            pltpu.VMEM((2 * t, DIFF_VDIM), _F32),
        ],
        compiler_params=pltpu.CompilerParams(
            dimension_semantics=("arbitrary", "arbitrary", "arbitrary"),
            vmem_limit_bytes=VMEM_LIMIT_BYTES),
        name="diff_attn",
    )(lam_vecs, q, k, v, subg)


def _post_kernel(ya_ref, yb_ref, x_ref, wo_ref, gpm_ref, gpf_ref, wg_ref, wu_ref, wd_ref,
                 gpo_ref, o_ref, acc_sc):
    mix = (_dot(ya_ref[...], wo_ref[0:GMLP_WIDTH, :])
           + _dot(yb_ref[...], wo_ref[GMLP_WIDTH:GMLP_WIDTH + DIFF_WIDTH, :]))
    x1 = x_ref[...] + _rms(mix, gpm_ref[...])
    h = _rms(x1, gpf_ref[...]).astype(_BF16)

    acc_sc[...] = jnp.zeros(acc_sc.shape, _F32)

    def body(c, carry):
        gate = _dot(h, wg_ref[c])
        up = _dot(h, wu_ref[c])
        act = (gate * jax.nn.sigmoid(gate) * up).astype(_BF16)
        acc_sc[...] += _dot(act, wd_ref[c])
        return carry

    lax.fori_loop(0, wg_ref.shape[0], body, 0)
    o_ref[...] = x1 + _rms(acc_sc[...], gpo_ref[...])


def _post(ya, yb, x, wo, gpm, gpf, wg, wu, wd, gpo):
    n, d = x.shape
    tm = ROW_TILE
    nc = wg.shape[0]
    row = lambda i: (i, 0)
    fixed2 = lambda i: (0, 0)
    fixed3 = lambda i: (0, 0, 0)
    once = dict(pipeline_mode=pl.Buffered(1))
    return pl.pallas_call(
        _post_kernel,
        grid=(n // tm,),
        in_specs=[
            pl.BlockSpec((tm, GMLP_WIDTH), row),
            pl.BlockSpec((tm, DIFF_WIDTH), row),
            pl.BlockSpec((tm, d), row),
            pl.BlockSpec((GMLP_WIDTH + DIFF_WIDTH, d), fixed2, **once),
            pl.BlockSpec((1, d), fixed2),
            pl.BlockSpec((1, d), fixed2),
            pl.BlockSpec((nc, d, FFN_CHUNK), fixed3, **once),
            pl.BlockSpec((nc, d, FFN_CHUNK), fixed3, **once),
            pl.BlockSpec((nc, FFN_CHUNK, d), fixed3, **once),
            pl.BlockSpec((1, d), fixed2),
        ],
        out_specs=pl.BlockSpec((tm, d), row),
        out_shape=jax.ShapeDtypeStruct((n, d), _F32),
        scratch_shapes=[pltpu.VMEM((tm, d), _F32)],
        compiler_params=pltpu.CompilerParams(
            dimension_semantics=("arbitrary",), vmem_limit_bytes=VMEM_LIMIT_BYTES),
        name="post",
    )(ya, yb, x, wo, gpm, gpf, wg, wu, wd, gpo)


def _rope_tables(seq):
    half = DIFF_QK_DIM // 2
    inv = 1.0 / (ROPE_THETA ** (jnp.arange(0, DIFF_QK_DIM, 2, dtype=_F32) / DIFF_QK_DIM))
    ang = jnp.arange(seq, dtype=_F32)[:, None] * inv[None, :]
    cos, sin = jnp.cos(ang), jnp.sin(ang)
    zeros = jnp.zeros_like(sin)
    reps = LANES // DIFF_QK_DIM
    cos_t = jnp.tile(jnp.concatenate([cos, cos], axis=1), (1, reps))
    sa_t = jnp.tile(jnp.concatenate([-sin, zeros], axis=1), (1, reps))
    sb_t = jnp.tile(jnp.concatenate([zeros, sin], axis=1), (1, reps))
    del half
    return cos_t, sa_t, sb_t


def kernel(x, pre_mix_g, w_in, gmlp_ln_g, gmlp_ln_b, gmlp_ws, gmlp_b, lambda_q1, lambda_k1,
           lambda_q2, lambda_k2, subln_g, w_out, post_mix_g, pre_ffn_g, w_gate_up, w_down,
           post_ffn_g):
    batch, seq, d = x.shape
    depth = w_in.shape[0]
    assert d == D_MODEL and seq % ROW_TILE == 0 and seq % ATTN_TILE == 0
    assert FFN_HIDDEN % FFN_CHUNK == 0
    nc = FFN_HIDDEN // FFN_CHUNK
    cos_t, sa_t, sb_t = _rope_tables(seq)
    xf = x.reshape(batch * seq, d)
    row = lambda a: a.reshape(1, -1)
    for l in range(depth):
        lam_init = 0.8 - 0.6 * math.exp(-0.3 * l)
        bias = jnp.broadcast_to(gmlp_b[l][:, :, None],
                                (GMLP_GROUPS, GMLP_BLOCK, GMLP_GROUP_DIM))
        ya, q, k, v = _inproj(xf, row(pre_mix_g[l]), w_in[l].astype(_BF16),
                              row(gmlp_ln_g[l]), row(gmlp_ln_b[l]), gmlp_ws[l], bias,
                              cos_t, sa_t, sb_t, seq)
        lam_vecs = jnp.stack([lambda_q1[l], lambda_k1[l], lambda_q2[l], lambda_k2[l]])
        yb = _attn(lam_vecs, q, k, v, row(subln_g[l]), batch, seq, lam_init)
        wgu = w_gate_up[l].astype(_BF16)
        wg = wgu[:, :FFN_HIDDEN].reshape(d, nc, FFN_CHUNK).transpose(1, 0, 2)
        wu = wgu[:, FFN_HIDDEN:].reshape(d, nc, FFN_CHUNK).transpose(1, 0, 2)
        wd = w_down[l].astype(_BF16).reshape(nc, FFN_CHUNK, d)
        xf = _post(ya, yb, xf, w_out[l].astype(_BF16), row(post_mix_g[l]), row(pre_ffn_g[l]),
                   wg, wu, wd, row(post_ffn_g[l]))
    return xf.reshape(batch, seq, d)
```

```python
import functools
import math

import jax
import jax.numpy as jnp
from jax import lax
from jax.experimental import pallas as pl
from jax.experimental.pallas import tpu as pltpu

D_MODEL = 1024
CHUNK = 64
GMLP_WIDTH = 512
GMLP_GROUPS = 4
GMLP_GROUP_DIM = 128
GMLP_BLOCK = 128
DIFF_WIDTH = 512
DIFF_HEADS = 4
DIFF_VDIM = 128
DIFF_QK_DIM = 64
ROPE_THETA = 10000.0
FFN_HIDDEN = 2816
IN_WIDTH = 2560
NORM_EPS = 1e-6
NEG_INF = -1e30

LANES = 128
MXU_COLS = 256
VMEM_LIMIT_BYTES = 56 * 1024 * 1024

ROW_TILE = 512
ATTN_TQ = 1024
FFN_CHUNK = 256

_BF16 = jnp.bfloat16
_F32 = jnp.float32


def _dot(a, b):
    return jnp.dot(a, b, preferred_element_type=_F32)


def _rms(x, g):
    return x * lax.rsqrt(jnp.mean(x * x, axis=-1, keepdims=True) + NORM_EPS) * g


def _gelu(x):
    return 0.5 * x * (1.0 + lax.erf(x * math.sqrt(0.5)))


def _inproj_kernel(x_ref, g_ref, w_ref, lng_ref, lnb_ref, ws_ref, bias_ref,
                   cos_ref, sa_ref, sb_ref, ya_ref, q_ref, k_ref, vt_ref):
    tm = x_ref.shape[0]
    h = _rms(x_ref[...], g_ref[...]).astype(_BF16)

    u = _gelu(_dot(h, w_ref[:, 0:GMLP_WIDTH]))
    vg = _gelu(_dot(h, w_ref[:, GMLP_WIDTH:2 * GMLP_WIDTH]))
    mu = jnp.mean(vg, axis=-1, keepdims=True)
    vc = vg - mu
    var = jnp.mean(vc * vc, axis=-1, keepdims=True)
    vln = (vc * lax.rsqrt(var + NORM_EPS) * lng_ref[...] + lnb_ref[...]).astype(_BF16)

    t_chunk = lax.broadcasted_iota(jnp.int32, (GMLP_BLOCK, GMLP_BLOCK), 0) // CHUNK
    s_chunk = lax.broadcasted_iota(jnp.int32, (GMLP_BLOCK, GMLP_BLOCK), 1) // CHUNK
    causal = s_chunk <= t_chunk
    for g in range(GMLP_GROUPS):
        w_g = jnp.where(causal, ws_ref[g], 0.0).astype(_BF16)
        cols = slice(g * GMLP_GROUP_DIM, (g + 1) * GMLP_GROUP_DIM)
        for r in range(tm // GMLP_BLOCK):
            rows = slice(r * GMLP_BLOCK, (r + 1) * GMLP_BLOCK)
            mixed = _dot(w_g, vln[rows, cols]) + bias_ref[g]
            ya_ref[rows, cols] = (u[rows, cols] * mixed).astype(_BF16)

    cos = cos_ref[...]
    sa = sa_ref[...]
    sb = sb_ref[...]

    def rope(z, scale):
        out = []
        for hb in range(z.shape[1] // LANES):
            blk = z[:, hb * LANES:(hb + 1) * LANES]
            rot = (blk * cos + pltpu.roll(blk, LANES - DIFF_QK_DIM // 2, 1) * sa
                   + pltpu.roll(blk, DIFF_QK_DIM // 2, 1) * sb)
            out.append((rot * scale).astype(_BF16))
        return out

    o_q = 2 * GMLP_WIDTH
    o_k = o_q + DIFF_WIDTH
    o_v = o_k + DIFF_WIDTH
    for hb, blk in enumerate(rope(_dot(h, w_ref[:, o_q:o_k]), DIFF_QK_DIM ** -0.5 * math.log2(math.e))):
        q_ref[:, hb * LANES:(hb + 1) * LANES] = blk
    for hb, blk in enumerate(rope(_dot(h, w_ref[:, o_k:o_v]), 1.0)):
        k_ref[:, hb * LANES:(hb + 1) * LANES] = blk
    vals = _dot(h, w_ref[:, o_v:IN_WIDTH])
    for hb in range(DIFF_HEADS):
        vt_ref[0, hb, 0] = vals[:, hb * DIFF_VDIM:(hb + 1) * DIFF_VDIM].T.astype(_BF16)


def _inproj(x, g, w, lng, lnb, ws, bias, cos, sa, sb, seq):
    n, d = x.shape
    tm = ROW_TILE
    seq_tiles = seq // tm
    row = lambda i: (i, 0)
    fixed2 = lambda i: (0, 0)
    fixed3 = lambda i: (0, 0, 0)
    pos = lambda i: (i % seq_tiles, 0)
    out_sds = jax.ShapeDtypeStruct((n, DIFF_WIDTH), _BF16)
    vt_sds = jax.ShapeDtypeStruct((n // seq, DIFF_HEADS, seq_tiles, DIFF_VDIM, tm), _BF16)
    vt_spec = pl.BlockSpec((1, DIFF_HEADS, 1, DIFF_VDIM, tm),
                           lambda i: (i // seq_tiles, 0, i % seq_tiles, 0, 0))
    return pl.pallas_call(
        _inproj_kernel,
        grid=(n // tm,),
        in_specs=[
            pl.BlockSpec((tm, d), row),
            pl.BlockSpec((1, d), fixed2),
            pl.BlockSpec((d, IN_WIDTH), fixed2),
            pl.BlockSpec((1, GMLP_WIDTH), fixed2),
            pl.BlockSpec((1, GMLP_WIDTH), fixed2),
            pl.BlockSpec((GMLP_GROUPS, GMLP_BLOCK, GMLP_BLOCK), fixed3),
            pl.BlockSpec((GMLP_GROUPS, GMLP_BLOCK, GMLP_GROUP_DIM), fixed3),
            pl.BlockSpec((tm, LANES), pos),
            pl.BlockSpec((tm, LANES), pos),
            pl.BlockSpec((tm, LANES), pos),
        ],
        out_specs=[pl.BlockSpec((tm, DIFF_WIDTH), row)] * 3 + [vt_spec],
        out_shape=[out_sds] * 3 + [vt_sds],
        compiler_params=pltpu.CompilerParams(
            dimension_semantics=("arbitrary",), vmem_limit_bytes=VMEM_LIMIT_BYTES),
        name="inproj",
    )(x, g, w, lng, lnb, ws, bias, cos, sa, sb)


def _attn_kernel(lam_ref, q_ref, k_ref, vt_ref, subg_ref, o_ref,
                 qqt_sc, s0_sc, s1_sc, mt0_sc, mt1_sc, p0_sc, p1_sc, a0_sc, a1_sc,
                 m_sc, l_sc, acc_sc, *, lam_init):
    t = q_ref.shape[0]
    tk = s0_sc.shape[0]
    i = pl.program_id(2)
    s_sc, mt_sc, p_sc, a_sc = (s0_sc, s1_sc), (mt0_sc, mt1_sc), (p0_sc, p1_sc), (a0_sc, a1_sc)

    qt = q_ref[...].astype(_F32).T
    sub = lax.broadcasted_iota(jnp.int32, qt.shape, 0)
    qqt_sc[:, 0:t] = jnp.where(sub < DIFF_QK_DIM, qt, 0.0).astype(_BF16)
    qqt_sc[:, t:2 * t] = jnp.where(sub >= DIFF_QK_DIM, qt, 0.0).astype(_BF16)
    m_sc[...] = jnp.full(m_sc.shape, NEG_INF, _F32)
    l_sc[...] = jnp.zeros(l_sc.shape, _F32)
    acc_sc[...] = jnp.zeros(acc_sc.shape, _F32)

    col_blocks = [slice(c * MXU_COLS, (c + 1) * MXU_COLS) for c in range(2 * t // MXU_COLS)]

    def qk(j, b, diag=None):
        start = pl.multiple_of(j * tk, tk)
        for cols in col_blocks:
            st = _dot(k_ref[pl.ds(start, tk), :], qqt_sc[:, cols])
            if diag is not None:
                kc = (lax.broadcasted_iota(jnp.int32, st.shape, 0) + diag * tk) // CHUNK
                qc = (lax.broadcasted_iota(jnp.int32, st.shape, 1) + cols.start) % t // CHUNK
                st = jnp.where(kc <= qc, st, NEG_INF)
            s_sc[b][:, cols] = st
            mt_sc[b][:, cols] = jnp.max(st, axis=0, keepdims=True)

    def softmax(b):
        for cols in col_blocks:
            m_prev = m_sc[:, cols]
            m_new = jnp.maximum(m_prev, mt_sc[b][:, cols])
            alpha = jnp.exp2(m_prev - m_new)
            p = jnp.exp2(s_sc[b][:, cols] - m_new)
            l_sc[:, cols] = alpha * l_sc[:, cols] + jnp.sum(p, axis=0, keepdims=True)
            m_sc[:, cols] = m_new
            a_sc[b][:, cols] = alpha
            p_sc[b][:, cols] = p.astype(_BF16)

    def pv(j, b):
        for cols in col_blocks:
            upd = _dot(vt_ref[0, 0, j], p_sc[b][:, cols])
            acc_sc[:, cols] = a_sc[b][:, cols] * acc_sc[:, cols] + upd

    n_full = 2 * i

    @pl.when(i > 0)
    def _():
        qk(0, 0)
        qk(1, 1)
        softmax(0)

        def pair(u, carry):
            j = 2 * u + 2
            qk(j, 0)
            softmax(1)
            pv(j - 2, 0)
            qk(j + 1, 1)
            softmax(0)
            pv(j - 1, 1)
            return carry

        lax.fori_loop(0, i - 1, pair, 0)
        qk(n_full, 0, diag=0)
        softmax(1)
        pv(n_full - 2, 0)
        qk(n_full + 1, 1, diag=1)
        softmax(0)
        pv(n_full - 1, 1)
        softmax(1)
        pv(n_full, 0)
        pv(n_full + 1, 1)

    @pl.when(i == 0)
    def _():
        qk(0, 0, diag=0)
        qk(1, 1, diag=1)
        softmax(0)
        softmax(1)
        pv(0, 0)
        pv(1, 1)

    lam_v = lam_ref[...]
    lam = (jnp.exp(jnp.sum(lam_v[0:1] * lam_v[1:2], axis=-1, keepdims=True))
           - jnp.exp(jnp.sum(lam_v[2:3] * lam_v[3:4], axis=-1, keepdims=True)) + lam_init)
    ot = acc_sc[:, 0:t] / l_sc[:, 0:t] - lam * (acc_sc[:, t:2 * t] / l_sc[:, t:2 * t])
    ot = ot * lax.rsqrt(jnp.mean(ot * ot, axis=0, keepdims=True) + NORM_EPS)
    o_ref[...] = (ot.T * subg_ref[...] * (1.0 - lam_init)).astype(_BF16)


def _attn(lam_vecs, q, k, vt, subg, batch, seq, lam_init):
    t = ATTN_TQ
    tk = t // 2
    nq = seq // t
    assert vt.shape[2:] == (seq // tk, DIFF_VDIM, tk)
    stat = pltpu.VMEM((1, 2 * t), _F32)
    return pl.pallas_call(
        functools.partial(_attn_kernel, lam_init=lam_init),
        grid=(batch, DIFF_HEADS, nq),
        in_specs=[
            pl.BlockSpec((4, DIFF_QK_DIM), lambda b, h, i: (0, 0)),
            pl.BlockSpec((t, LANES), lambda b, h, i: (b * nq + i, h)),
            pl.BlockSpec((seq, LANES), lambda b, h, i: (b, h)),
            pl.BlockSpec((1, 1, seq // tk, DIFF_VDIM, tk), lambda b, h, i: (b, h, 0, 0, 0)),
            pl.BlockSpec((1, DIFF_VDIM), lambda b, h, i: (0, 0)),
        ],
        out_specs=pl.BlockSpec((t, LANES), lambda b, h, i: (b * nq + i, h)),
        out_shape=jax.ShapeDtypeStruct(q.shape, _BF16),
        scratch_shapes=[
            pltpu.VMEM((LANES, 2 * t), _BF16),
            pltpu.VMEM((tk, 2 * t), _F32),
            pltpu.VMEM((tk, 2 * t), _F32),
            stat, stat,
            pltpu.VMEM((tk, 2 * t), _BF16),
            pltpu.VMEM((tk, 2 * t), _BF16),
            stat, stat, stat, stat,
            pltpu.VMEM((DIFF_VDIM, 2 * t), _F32),
        ],
        compiler_params=pltpu.CompilerParams(
            dimension_semantics=("arbitrary", "arbitrary", "arbitrary"),
            vmem_limit_bytes=VMEM_LIMIT_BYTES),
        name="diff_attn",
    )(lam_vecs, q, k, vt, subg)


def _post_kernel(ya_ref, yb_ref, x_ref, wo_ref, gpm_ref, gpf_ref, wg_ref, wu_ref, wd_ref,
                 gpo_ref, o_ref, acc_sc):
    mix = (_dot(ya_ref[...], wo_ref[0:GMLP_WIDTH, :])
           + _dot(yb_ref[...], wo_ref[GMLP_WIDTH:GMLP_WIDTH + DIFF_WIDTH, :]))
    x1 = x_ref[...] + _rms(mix, gpm_ref[...])
    h = _rms(x1, gpf_ref[...]).astype(_BF16)

    acc_sc[...] = jnp.zeros(acc_sc.shape, _F32)

    def body(c, carry):
        gate = _dot(h, wg_ref[c])
        up = _dot(h, wu_ref[c])
        act = (gate * jax.nn.sigmoid(gate) * up).astype(_BF16)
        acc_sc[...] += _dot(act, wd_ref[c])
        return carry

    lax.fori_loop(0, wg_ref.shape[0], body, 0)
    o_ref[...] = x1 + _rms(acc_sc[...], gpo_ref[...])


def _post(ya, yb, x, wo, gpm, gpf, wg, wu, wd, gpo):
    n, d = x.shape
    tm = ROW_TILE
    nc = wg.shape[0]
    row = lambda i: (i, 0)
    fixed2 = lambda i: (0, 0)
    fixed3 = lambda i: (0, 0, 0)
    once = dict(pipeline_mode=pl.Buffered(1))
    return pl.pallas_call(
        _post_kernel,
        grid=(n // tm,),
        in_specs=[
            pl.BlockSpec((tm, GMLP_WIDTH), row),
            pl.BlockSpec((tm, DIFF_WIDTH), row),
            pl.BlockSpec((tm, d), row),
            pl.BlockSpec((GMLP_WIDTH + DIFF_WIDTH, d), fixed2, **once),
            pl.BlockSpec((1, d), fixed2),
            pl.BlockSpec((1, d), fixed2),
            pl.BlockSpec((nc, d, FFN_CHUNK), fixed3, **once),
            pl.BlockSpec((nc, d, FFN_CHUNK), fixed3, **once),
            pl.BlockSpec((nc, FFN_CHUNK, d), fixed3, **once),
            pl.BlockSpec((1, d), fixed2),
        ],
        out_specs=pl.BlockSpec((tm, d), row),
        out_shape=jax.ShapeDtypeStruct((n, d), _F32),
        scratch_shapes=[pltpu.VMEM((tm, d), _F32)],
        compiler_params=pltpu.CompilerParams(
            dimension_semantics=("arbitrary",), vmem_limit_bytes=VMEM_LIMIT_BYTES),
        name="post",
    )(ya, yb, x, wo, gpm, gpf, wg, wu, wd, gpo)


def _rope_tables(seq):
    half = DIFF_QK_DIM // 2
    inv = 1.0 / (ROPE_THETA ** (jnp.arange(0, DIFF_QK_DIM, 2, dtype=_F32) / DIFF_QK_DIM))
    ang = jnp.arange(seq, dtype=_F32)[:, None] * inv[None, :]
    cos, sin = jnp.cos(ang), jnp.sin(ang)
    zeros = jnp.zeros_like(sin)
    reps = LANES // DIFF_QK_DIM
    cos_t = jnp.tile(jnp.concatenate([cos, cos], axis=1), (1, reps))
    sa_t = jnp.tile(jnp.concatenate([-sin, zeros], axis=1), (1, reps))
    sb_t = jnp.tile(jnp.concatenate([zeros, sin], axis=1), (1, reps))
    del half
    return cos_t, sa_t, sb_t


def kernel(x, pre_mix_g, w_in, gmlp_ln_g, gmlp_ln_b, gmlp_ws, gmlp_b, lambda_q1, lambda_k1,
           lambda_q2, lambda_k2, subln_g, w_out, post_mix_g, pre_ffn_g, w_gate_up, w_down,
           post_ffn_g):
    batch, seq, d = x.shape
    depth = w_in.shape[0]
    assert d == D_MODEL and seq % ROW_TILE == 0 and seq % ATTN_TQ == 0
    assert FFN_HIDDEN % FFN_CHUNK == 0
    nc = FFN_HIDDEN // FFN_CHUNK
    cos_t, sa_t, sb_t = _rope_tables(seq)
    xf = x.reshape(batch * seq, d)
    row = lambda a: a.reshape(1, -1)
    for l in range(depth):
        lam_init = 0.8 - 0.6 * math.exp(-0.3 * l)
        bias = jnp.broadcast_to(gmlp_b[l][:, :, None],
                                (GMLP_GROUPS, GMLP_BLOCK, GMLP_GROUP_DIM))
        ya, q, k, v = _inproj(xf, row(pre_mix_g[l]), w_in[l].astype(_BF16),
                              row(gmlp_ln_g[l]), row(gmlp_ln_b[l]), gmlp_ws[l], bias,
                              cos_t, sa_t, sb_t, seq)
        lam_vecs = jnp.stack([lambda_q1[l], lambda_k1[l], lambda_q2[l], lambda_k2[l]])
        yb = _attn(lam_vecs, q, k, v, row(subln_g[l]), batch, seq, lam_init)
        wgu = w_gate_up[l].astype(_BF16)
        wg = wgu[:, :FFN_HIDDEN].reshape(d, nc, FFN_CHUNK).transpose(1, 0, 2)
        wu = wgu[:, FFN_HIDDEN:].reshape(d, nc, FFN_CHUNK).transpose(1, 0, 2)
        wd = w_down[l].astype(_BF16).reshape(nc, FFN_CHUNK, d)
        xf = _post(ya, yb, xf, w_out[l].astype(_BF16), row(post_mix_g[l]), row(pre_ffn_g[l]),
                   wg, wu, wd, row(post_ffn_g[l]))
    return xf.reshape(batch, seq, d)
```

```python
import functools
import math

import jax
import jax.numpy as jnp
from jax import lax
from jax.experimental import pallas as pl
from jax.experimental.pallas import tpu as pltpu

D_MODEL = 1024
CHUNK = 64
GMLP_WIDTH = 512
GMLP_GROUPS = 4
GMLP_GROUP_DIM = 128
GMLP_BLOCK = 128
DIFF_WIDTH = 512
DIFF_HEADS = 4
DIFF_VDIM = 128
DIFF_QK_DIM = 64
ROPE_THETA = 10000.0
FFN_HIDDEN = 2816
IN_WIDTH = 2560
NORM_EPS = 1e-6
NEG_INF = -1e30

LANES = 128
MXU_COLS = 256
VMEM_LIMIT_BYTES = 56 * 1024 * 1024

ROW_TILE = 512
ATTN_TQ = 1024
FFN_CHUNK = 256

_BF16 = jnp.bfloat16
_F32 = jnp.float32


def _dot(a, b):
    return jnp.dot(a, b, preferred_element_type=_F32)


def _rms(x, g):
    return x * lax.rsqrt(jnp.mean(x * x, axis=-1, keepdims=True) + NORM_EPS) * g


def _gelu(x):
    return 0.5 * x * (1.0 + lax.erf(x * math.sqrt(0.5)))


def _inproj_kernel(x_ref, g_ref, w_ref, lng_ref, lnb_ref, ws_ref, bias_ref,
                   cos_ref, sa_ref, sb_ref, ya_ref, q_ref, k_ref, vt_ref):
    tm = x_ref.shape[0]
    h = _rms(x_ref[...], g_ref[...]).astype(_BF16)

    u = _gelu(_dot(h, w_ref[:, 0:GMLP_WIDTH]))
    vg = _gelu(_dot(h, w_ref[:, GMLP_WIDTH:2 * GMLP_WIDTH]))
    mu = jnp.mean(vg, axis=-1, keepdims=True)
    vc = vg - mu
    var = jnp.mean(vc * vc, axis=-1, keepdims=True)
    vln = (vc * lax.rsqrt(var + NORM_EPS) * lng_ref[...] + lnb_ref[...]).astype(_BF16)

    t_chunk = lax.broadcasted_iota(jnp.int32, (GMLP_BLOCK, GMLP_BLOCK), 0) // CHUNK
    s_chunk = lax.broadcasted_iota(jnp.int32, (GMLP_BLOCK, GMLP_BLOCK), 1) // CHUNK
    causal = s_chunk <= t_chunk
    for g in range(GMLP_GROUPS):
        w_g = jnp.where(causal, ws_ref[g], 0.0).astype(_BF16)
        cols = slice(g * GMLP_GROUP_DIM, (g + 1) * GMLP_GROUP_DIM)
        for r in range(tm // GMLP_BLOCK):
            rows = slice(r * GMLP_BLOCK, (r + 1) * GMLP_BLOCK)
            mixed = _dot(w_g, vln[rows, cols]) + bias_ref[g]
            ya_ref[rows, cols] = (u[rows, cols] * mixed).astype(_BF16)

    cos = cos_ref[...]
    sa = sa_ref[...]
    sb = sb_ref[...]

    def rope(z, scale):
        out = []
        for hb in range(z.shape[1] // LANES):
            blk = z[:, hb * LANES:(hb + 1) * LANES]
            rot = (blk * cos + pltpu.roll(blk, LANES - DIFF_QK_DIM // 2, 1) * sa
                   + pltpu.roll(blk, DIFF_QK_DIM // 2, 1) * sb)
            out.append((rot * scale).astype(_BF16))
        return out

    o_q = 2 * GMLP_WIDTH
    o_k = o_q + DIFF_WIDTH
    o_v = o_k + DIFF_WIDTH
    for hb, blk in enumerate(rope(_dot(h, w_ref[:, o_q:o_k]), DIFF_QK_DIM ** -0.5 * math.log2(math.e))):
        q_ref[:, hb * LANES:(hb + 1) * LANES] = blk
    for hb, blk in enumerate(rope(_dot(h, w_ref[:, o_k:o_v]), 1.0)):
        k_ref[:, hb * LANES:(hb + 1) * LANES] = blk
    vals = _dot(h, w_ref[:, o_v:IN_WIDTH])
    for hb in range(DIFF_HEADS):
        vt_ref[0, hb, 0] = vals[:, hb * DIFF_VDIM:(hb + 1) * DIFF_VDIM].T.astype(_BF16)


def _inproj(x, g, w, lng, lnb, ws, bias, cos, sa, sb, seq):
    n, d = x.shape
    tm = ROW_TILE
    seq_tiles = seq // tm
    row = lambda i: (i, 0)
    fixed2 = lambda i: (0, 0)
    fixed3 = lambda i: (0, 0, 0)
    pos = lambda i: (i % seq_tiles, 0)
    out_sds = jax.ShapeDtypeStruct((n, DIFF_WIDTH), _BF16)
    vt_sds = jax.ShapeDtypeStruct((n // seq, DIFF_HEADS, seq_tiles, DIFF_VDIM, tm), _BF16)
    vt_spec = pl.BlockSpec((1, DIFF_HEADS, 1, DIFF_VDIM, tm),
                           lambda i: (i // seq_tiles, 0, i % seq_tiles, 0, 0))
    return pl.pallas_call(
        _inproj_kernel,
        grid=(n // tm,),
        in_specs=[
            pl.BlockSpec((tm, d), row),
            pl.BlockSpec((1, d), fixed2),
            pl.BlockSpec((d, IN_WIDTH), fixed2),
            pl.BlockSpec((1, GMLP_WIDTH), fixed2),
            pl.BlockSpec((1, GMLP_WIDTH), fixed2),
            pl.BlockSpec((GMLP_GROUPS, GMLP_BLOCK, GMLP_BLOCK), fixed3),
            pl.BlockSpec((GMLP_GROUPS, GMLP_BLOCK, GMLP_GROUP_DIM), fixed3),
            pl.BlockSpec((tm, LANES), pos),
            pl.BlockSpec((tm, LANES), pos),
            pl.BlockSpec((tm, LANES), pos),
        ],
        out_specs=[pl.BlockSpec((tm, DIFF_WIDTH), row)] * 3 + [vt_spec],
        out_shape=[out_sds] * 3 + [vt_sds],
        compiler_params=pltpu.CompilerParams(
            dimension_semantics=("arbitrary",), vmem_limit_bytes=VMEM_LIMIT_BYTES),
        name="inproj",
    )(x, g, w, lng, lnb, ws, bias, cos, sa, sb)


def _attn_kernel(lam_ref, q_ref, k_ref, vt_ref, subg_ref, o_ref,
                 qqt_sc, s0_sc, s1_sc, mt0_sc, mt1_sc, m_sc, l_sc, acc_sc, *, lam_init):
    t = q_ref.shape[0]
    tk = s0_sc.shape[0]
    i = pl.program_id(2)
    s_sc, mt_sc = (s0_sc, s1_sc), (mt0_sc, mt1_sc)

    qt = q_ref[...].astype(_F32).T
    sub = lax.broadcasted_iota(jnp.int32, qt.shape, 0)
    qqt_sc[:, 0:t] = jnp.where(sub < DIFF_QK_DIM, qt, 0.0).astype(_BF16)
    qqt_sc[:, t:2 * t] = jnp.where(sub >= DIFF_QK_DIM, qt, 0.0).astype(_BF16)
    m_sc[...] = jnp.full(m_sc.shape, NEG_INF, _F32)
    l_sc[...] = jnp.zeros(l_sc.shape, _F32)
    acc_sc[...] = jnp.zeros(acc_sc.shape, _F32)

    col_blocks = [slice(c * MXU_COLS, (c + 1) * MXU_COLS) for c in range(2 * t // MXU_COLS)]

    def visible_keys(cols, diag):
        if diag is None:
            return tk
        return min(max(cols.start % t + MXU_COLS - diag * tk, 0), tk)

    def qk(j, b, diag=None):
        start = pl.multiple_of(j * tk, tk)
        for cols in col_blocks:
            nk = visible_keys(cols, diag)
            if nk == 0:
                continue
            st = _dot(k_ref[pl.ds(start, nk), :], qqt_sc[:, cols])
            if diag is not None:
                kc = (lax.broadcasted_iota(jnp.int32, st.shape, 0) + diag * tk) // CHUNK
                qc = (lax.broadcasted_iota(jnp.int32, st.shape, 1) + cols.start) % t // CHUNK
                st = jnp.where(kc <= qc, st, NEG_INF)
            s_sc[b][0:nk, cols] = st
            mt_sc[b][:, cols] = jnp.max(st, axis=0, keepdims=True)

    def softmax_pv(j, b, diag=None):
        for cols in col_blocks:
            nk = visible_keys(cols, diag)
            if nk == 0:
                continue
            m_prev = m_sc[:, cols]
            m_new = jnp.maximum(m_prev, mt_sc[b][:, cols])
            alpha = jnp.exp2(m_prev - m_new)
            p = jnp.exp2(s_sc[b][0:nk, cols] - m_new)
            l_sc[:, cols] = alpha * l_sc[:, cols] + jnp.sum(p, axis=0, keepdims=True)
            m_sc[:, cols] = m_new
            upd = _dot(vt_ref[0, 0, j, :, 0:nk], p.astype(_BF16))
            acc_sc[:, cols] = alpha * acc_sc[:, cols] + upd

    n_full = 2 * i

    @pl.when(i > 0)
    def _():
        qk(0, 0)
        qk(1, 1)
        softmax_pv(0, 0)

        def pair(u, carry):
            j = 2 * u + 2
            qk(j, 0)
            softmax_pv(j - 1, 1)
            qk(j + 1, 1)
            softmax_pv(j, 0)
            return carry

        lax.fori_loop(0, i - 1, pair, 0)
        qk(n_full, 0, diag=0)
        softmax_pv(n_full - 1, 1)
        qk(n_full + 1, 1, diag=1)
        softmax_pv(n_full, 0, diag=0)
        softmax_pv(n_full + 1, 1, diag=1)

    @pl.when(i == 0)
    def _():
        qk(0, 0, diag=0)
        qk(1, 1, diag=1)
        softmax_pv(0, 0, diag=0)
        softmax_pv(1, 1, diag=1)

    lam_v = lam_ref[...]
    lam = (jnp.exp(jnp.sum(lam_v[0:1] * lam_v[1:2], axis=-1, keepdims=True))
           - jnp.exp(jnp.sum(lam_v[2:3] * lam_v[3:4], axis=-1, keepdims=True)) + lam_init)
    ot = acc_sc[:, 0:t] / l_sc[:, 0:t] - lam * (acc_sc[:, t:2 * t] / l_sc[:, t:2 * t])
    ot = ot * lax.rsqrt(jnp.mean(ot * ot, axis=0, keepdims=True) + NORM_EPS)
    o_ref[...] = (ot.T * subg_ref[...] * (1.0 - lam_init)).astype(_BF16)


def _attn(lam_vecs, q, k, vt, subg, batch, seq, lam_init):
    t = ATTN_TQ
    tk = t // 2
    nq = seq // t
    assert vt.shape[2:] == (seq // tk, DIFF_VDIM, tk)
    stat = pltpu.VMEM((1, 2 * t), _F32)
    return pl.pallas_call(
        functools.partial(_attn_kernel, lam_init=lam_init),
        grid=(batch, DIFF_HEADS, nq),
        in_specs=[
            pl.BlockSpec((4, DIFF_QK_DIM), lambda b, h, i: (0, 0)),
            pl.BlockSpec((t, LANES), lambda b, h, i: (b * nq + i, h)),
            pl.BlockSpec((seq, LANES), lambda b, h, i: (b, h)),
            pl.BlockSpec((1, 1, seq // tk, DIFF_VDIM, tk), lambda b, h, i: (b, h, 0, 0, 0)),
            pl.BlockSpec((1, DIFF_VDIM), lambda b, h, i: (0, 0)),
        ],
        out_specs=pl.BlockSpec((t, LANES), lambda b, h, i: (b * nq + i, h)),
        out_shape=jax.ShapeDtypeStruct(q.shape, _BF16),
        scratch_shapes=[
            pltpu.VMEM((LANES, 2 * t), _BF16),
            pltpu.VMEM((tk, 2 * t), _F32),
            pltpu.VMEM((tk, 2 * t), _F32),
            stat, stat, stat, stat,
            pltpu.VMEM((DIFF_VDIM, 2 * t), _F32),
        ],
        compiler_params=pltpu.CompilerParams(
            dimension_semantics=("arbitrary", "arbitrary", "arbitrary"),
            vmem_limit_bytes=VMEM_LIMIT_BYTES),
        name="diff_attn",
    )(lam_vecs, q, k, vt, subg)


def _post_kernel(ya_ref, yb_ref, x_ref, wo_ref, gpm_ref, gpf_ref, wgu_ref, wd_ref,
                 gpo_ref, o_ref, act_sc):
    mix = (_dot(ya_ref[...], wo_ref[0:GMLP_WIDTH, :])
           + _dot(yb_ref[...], wo_ref[GMLP_WIDTH:GMLP_WIDTH + DIFF_WIDTH, :]))
    x1 = x_ref[...] + _rms(mix, gpm_ref[...])
    h = _rms(x1, gpf_ref[...]).astype(_BF16)

    for lo in range(0, FFN_HIDDEN, FFN_CHUNK):
        gate = _dot(h, wgu_ref[:, lo:lo + FFN_CHUNK])
        up = _dot(h, wgu_ref[:, FFN_HIDDEN + lo:FFN_HIDDEN + lo + FFN_CHUNK])
        act_sc[:, lo:lo + FFN_CHUNK] = (gate * jax.nn.sigmoid(gate) * up).astype(_BF16)
    y = _dot(act_sc[...], wd_ref[...])
    o_ref[...] = x1 + _rms(y, gpo_ref[...])


def _post(ya, yb, x, wo, gpm, gpf, wgu, wd, gpo):
    n, d = x.shape
    tm = ROW_TILE
    row = lambda i: (i, 0)
    fixed2 = lambda i: (0, 0)
    once = dict(pipeline_mode=pl.Buffered(1))
    return pl.pallas_call(
        _post_kernel,
        grid=(n // tm,),
        in_specs=[
            pl.BlockSpec((tm, GMLP_WIDTH), row),
            pl.BlockSpec((tm, DIFF_WIDTH), row),
            pl.BlockSpec((tm, d), row),
            pl.BlockSpec((GMLP_WIDTH + DIFF_WIDTH, d), fixed2, **once),
            pl.BlockSpec((1, d), fixed2),
            pl.BlockSpec((1, d), fixed2),
            pl.BlockSpec((d, 2 * FFN_HIDDEN), fixed2, **once),
            pl.BlockSpec((FFN_HIDDEN, d), fixed2, **once),
            pl.BlockSpec((1, d), fixed2),
        ],
        out_specs=pl.BlockSpec((tm, d), row),
        out_shape=jax.ShapeDtypeStruct((n, d), _F32),
        scratch_shapes=[pltpu.VMEM((tm, FFN_HIDDEN), _BF16)],
        compiler_params=pltpu.CompilerParams(
            dimension_semantics=("arbitrary",), vmem_limit_bytes=VMEM_LIMIT_BYTES),
        name="post",
    )(ya, yb, x, wo, gpm, gpf, wgu, wd, gpo)


def _rope_tables(seq):
    half = DIFF_QK_DIM // 2
    inv = 1.0 / (ROPE_THETA ** (jnp.arange(0, DIFF_QK_DIM, 2, dtype=_F32) / DIFF_QK_DIM))
    ang = jnp.arange(seq, dtype=_F32)[:, None] * inv[None, :]
    cos, sin = jnp.cos(ang), jnp.sin(ang)
    zeros = jnp.zeros_like(sin)
    reps = LANES // DIFF_QK_DIM
    cos_t = jnp.tile(jnp.concatenate([cos, cos], axis=1), (1, reps))
    sa_t = jnp.tile(jnp.concatenate([-sin, zeros], axis=1), (1, reps))
    sb_t = jnp.tile(jnp.concatenate([zeros, sin], axis=1), (1, reps))
    del half
    return cos_t, sa_t, sb_t


def kernel(x, pre_mix_g, w_in, gmlp_ln_g, gmlp_ln_b, gmlp_ws, gmlp_b, lambda_q1, lambda_k1,
           lambda_q2, lambda_k2, subln_g, w_out, post_mix_g, pre_ffn_g, w_gate_up, w_down,
           post_ffn_g):
    batch, seq, d = x.shape
    depth = w_in.shape[0]
    assert d == D_MODEL and seq % ROW_TILE == 0 and seq % ATTN_TQ == 0
    assert FFN_HIDDEN % FFN_CHUNK == 0 and ATTN_TQ == 2 * ROW_TILE
    cos_t, sa_t, sb_t = _rope_tables(seq)
    xf = x.reshape(batch * seq, d)
    row = lambda a: a.reshape(1, -1)
    for l in range(depth):
        lam_init = 0.8 - 0.6 * math.exp(-0.3 * l)
        bias = jnp.broadcast_to(gmlp_b[l][:, :, None],
                                (GMLP_GROUPS, GMLP_BLOCK, GMLP_GROUP_DIM))
        ya, q, k, vt = _inproj(xf, row(pre_mix_g[l]), w_in[l].astype(_BF16),
                               row(gmlp_ln_g[l]), row(gmlp_ln_b[l]), gmlp_ws[l], bias,
                               cos_t, sa_t, sb_t, seq)
        lam_vecs = jnp.stack([lambda_q1[l], lambda_k1[l], lambda_q2[l], lambda_k2[l]])
        yb = _attn(lam_vecs, q, k, vt, row(subln_g[l]), batch, seq, lam_init)
        xf = _post(ya, yb, xf, w_out[l].astype(_BF16), row(post_mix_g[l]), row(pre_ffn_g[l]),
                   w_gate_up[l].astype(_BF16), w_down[l].astype(_BF16), row(post_ffn_g[l]))
    return xf.reshape(batch, seq, d)
```

```python
import functools
import math

import jax
import jax.numpy as jnp
from jax import lax
from jax.experimental import pallas as pl
from jax.experimental.pallas import tpu as pltpu

D_MODEL = 1024
CHUNK = 64
GMLP_WIDTH = 512
GMLP_GROUPS = 4
GMLP_GROUP_DIM = 128
GMLP_BLOCK = 128
DIFF_WIDTH = 512
DIFF_HEADS = 4
DIFF_VDIM = 128
DIFF_QK_DIM = 64
ROPE_THETA = 10000.0
FFN_HIDDEN = 2816
IN_WIDTH = 2560
NORM_EPS = 1e-6
NEG_INF = -1e30
ONE_PASS_MAX_GAP = 32.0
QK_LOOKAHEAD = 3

LANES = 128
MXU_COLS = 256
VMEM_LIMIT_BYTES = 56 * 1024 * 1024

ROW_TILE = 512
ATTN_TQ = 1024
FFN_CHUNK = 256
POST_SUB_ROWS = 256

_BF16 = jnp.bfloat16
_F32 = jnp.float32


def _dot(a, b):
    return jnp.dot(a, b, preferred_element_type=_F32)


def _rms(x, g):
    return x * lax.rsqrt(jnp.mean(x * x, axis=-1, keepdims=True) + NORM_EPS) * g


def _gelu(x):
    return 0.5 * x * (1.0 + lax.erf(x * math.sqrt(0.5)))


def _inproj_kernel(x_ref, g_ref, w_ref, lng_ref, lnb_ref, ws_ref, bias_ref,
                   cos_ref, sa_ref, sb_ref, ya_ref, q_ref, k_ref, vt_ref):
    tm = x_ref.shape[0]
    h = _rms(x_ref[...], g_ref[...]).astype(_BF16)

    vg = _gelu(_dot(h, w_ref[:, GMLP_WIDTH:2 * GMLP_WIDTH]))
    mu = jnp.mean(vg, axis=-1, keepdims=True)
    vc = vg - mu
    var = jnp.mean(vc * vc, axis=-1, keepdims=True)
    vln = (vc * lax.rsqrt(var + NORM_EPS) * lng_ref[...] + lnb_ref[...]).astype(_BF16)
    u = _gelu(_dot(h, w_ref[:, 0:GMLP_WIDTH]))

    cos = cos_ref[...]
    sa = sa_ref[...]
    sb = sb_ref[...]

    def rope(z, scale):
        out = []
        for hb in range(z.shape[1] // LANES):
            blk = z[:, hb * LANES:(hb + 1) * LANES]
            rot = (blk * cos + pltpu.roll(blk, LANES - DIFF_QK_DIM // 2, 1) * sa
                   + pltpu.roll(blk, DIFF_QK_DIM // 2, 1) * sb)
            out.append((rot * scale).astype(_BF16))
        return out

    o_q = 2 * GMLP_WIDTH
    o_k = o_q + DIFF_WIDTH
    o_v = o_k + DIFF_WIDTH
    for hb, blk in enumerate(rope(_dot(h, w_ref[:, o_q:o_k]), DIFF_QK_DIM ** -0.5 * math.log2(math.e))):
        q_ref[:, hb * LANES:(hb + 1) * LANES] = blk
    for hb, blk in enumerate(rope(_dot(h, w_ref[:, o_k:o_v]), 1.0)):
        k_ref[:, hb * LANES:(hb + 1) * LANES] = blk
    vals = _dot(h, w_ref[:, o_v:IN_WIDTH])
    for hb in range(DIFF_HEADS):
        vt_ref[0, hb, 0] = vals[:, hb * DIFF_VDIM:(hb + 1) * DIFF_VDIM].T.astype(_BF16)

    t_chunk = lax.broadcasted_iota(jnp.int32, (GMLP_BLOCK, GMLP_BLOCK), 0) // CHUNK
    s_chunk = lax.broadcasted_iota(jnp.int32, (GMLP_BLOCK, GMLP_BLOCK), 1) // CHUNK
    causal = s_chunk <= t_chunk
    for g in range(GMLP_GROUPS):
        w_g = jnp.where(causal, ws_ref[g], 0.0).astype(_BF16)
        cols = slice(g * GMLP_GROUP_DIM, (g + 1) * GMLP_GROUP_DIM)
        for r in range(tm // GMLP_BLOCK):
            rows = slice(r * GMLP_BLOCK, (r + 1) * GMLP_BLOCK)
            mixed = _dot(w_g, vln[rows, cols]) + bias_ref[g]
            ya_ref[rows, cols] = (u[rows, cols] * mixed).astype(_BF16)


def _layer_spec(shape, layer, **kwargs):
    index = (layer,) + (0,) * len(shape)
    return pl.BlockSpec((None,) + tuple(shape), lambda *_: index, **kwargs)


def _inproj(layer, x, g, w, lng, lnb, ws, bias, cos, sa, sb, seq):
    n, d = x.shape
    tm = ROW_TILE
    seq_tiles = seq // tm
    row = lambda i: (i, 0)
    par = functools.partial(_layer_spec, layer=layer)
    pos = lambda i: (i % seq_tiles, 0)
    out_sds = jax.ShapeDtypeStruct((n, DIFF_WIDTH), _BF16)
    vt_sds = jax.ShapeDtypeStruct((n // seq, DIFF_HEADS, seq_tiles, DIFF_VDIM, tm), _BF16)
    vt_spec = pl.BlockSpec((1, DIFF_HEADS, 1, DIFF_VDIM, tm),
                           lambda i: (i // seq_tiles, 0, i % seq_tiles, 0, 0))
    return pl.pallas_call(
        _inproj_kernel,
        grid=(n // tm,),
        in_specs=[
            pl.BlockSpec((tm, d), row),
            par((1, d)),
            par((d, IN_WIDTH)),
            par((1, GMLP_WIDTH)),
            par((1, GMLP_WIDTH)),
            par((GMLP_GROUPS, GMLP_BLOCK, GMLP_BLOCK)),
            par((GMLP_GROUPS, GMLP_BLOCK, GMLP_GROUP_DIM)),
            pl.BlockSpec((tm, LANES), pos),
            pl.BlockSpec((tm, LANES), pos),
            pl.BlockSpec((tm, LANES), pos),
        ],
        out_specs=[pl.BlockSpec((tm, DIFF_WIDTH), row)] * 3 + [vt_spec],
        out_shape=[out_sds] * 3 + [vt_sds],
        compiler_params=pltpu.CompilerParams(
            dimension_semantics=("arbitrary",), vmem_limit_bytes=VMEM_LIMIT_BYTES),
        name="inproj",
    )(x, g, w, lng, lnb, ws, bias, cos, sa, sb)


def _attn_kernel(lam_ref, q_ref, k_ref, vt_ref, subg_ref, o_ref,
                 qqt_sc, s0_sc, s1_sc, mt0_sc, mt1_sc, m_sc, l_sc, gap_sc, acc_sc, *, lam_init):
    t = q_ref.shape[0]
    nblk, tk, cw = s0_sc.shape
    i = pl.program_id(2)
    s_sc, mt_sc = (s0_sc, s1_sc), (mt0_sc, mt1_sc)
    lanes = lambda c: slice(c * cw, (c + 1) * cw)

    qt = q_ref[...].astype(_F32).T
    sub = lax.broadcasted_iota(jnp.int32, (LANES, cw), 0)
    for c in range(nblk):
        branch = sub < DIFF_QK_DIM if c < nblk // 2 else sub >= DIFF_QK_DIM
        qqt_sc[c] = jnp.where(branch, qt[:, lanes(c % (nblk // 2))], 0.0).astype(_BF16)
    m_sc[...] = jnp.full(m_sc.shape, NEG_INF, _F32)
    l_sc[...] = jnp.zeros(l_sc.shape, _F32)
    acc_sc[...] = jnp.zeros(acc_sc.shape, _F32)

    def visible_keys(c, diag):
        if diag is None:
            return tk
        return min(max(c % (nblk // 2) * cw + cw - diag * tk, 0), tk)

    def qk(j, b, diag=None):
        start = pl.multiple_of(j * tk, tk)
        for c in range(nblk):
            nk = visible_keys(c, diag)
            if nk == 0:
                continue
            st = _dot(k_ref[pl.ds(start, nk), :], qqt_sc[c])
            if diag is not None:
                kc = (lax.broadcasted_iota(jnp.int32, st.shape, 0) + diag * tk) // CHUNK
                qc = (lax.broadcasted_iota(jnp.int32, st.shape, 1) + c % (nblk // 2) * cw) // CHUNK
                st = jnp.where(kc <= qc, st, NEG_INF)
            s_sc[b][c, 0:nk, :] = st
            mt_sc[b][:, lanes(c)] = jnp.max(st, axis=0, keepdims=True)

    def softmax_pv(j, b, diag=None):
        for c in range(nblk):
            nk = visible_keys(c, diag)
            if nk == 0:
                continue
            m_prev = m_sc[:, lanes(c)]
            m_new = jnp.maximum(m_prev, mt_sc[b][:, lanes(c)])
            alpha = jnp.exp2(m_prev - m_new)
            p = jnp.exp2(s_sc[b][c, 0:nk, :] - m_new)
            l_sc[:, lanes(c)] = alpha * l_sc[:, lanes(c)] + jnp.sum(p, axis=0, keepdims=True)
            m_sc[:, lanes(c)] = m_new
            upd = _dot(vt_ref[0, 0, j, :, 0:nk], p.astype(_BF16))
            acc_sc[c] = alpha * acc_sc[c] + upd

    def stream(j, diag=None):
        start = pl.multiple_of(j * tk, tk)

        def scores(c):
            nk = visible_keys(c, diag)
            st = _dot(k_ref[pl.ds(start, nk), :], qqt_sc[c])
            if diag is not None:
                kc = (lax.broadcasted_iota(jnp.int32, st.shape, 0) + diag * tk) // CHUNK
                qc = (lax.broadcasted_iota(jnp.int32, st.shape, 1) + c % (nblk // 2) * cw) // CHUNK
                st = jnp.where(kc <= qc, st, NEG_INF)
            return st

        blocks = [c for c in range(nblk) if visible_keys(c, diag) > 0]
        ahead = [scores(c) for c in blocks[:QK_LOOKAHEAD]]
        for n, c in enumerate(blocks):
            nk = visible_keys(c, diag)
            st = ahead.pop(0)
            if n + QK_LOOKAHEAD < len(blocks):
                ahead.append(scores(blocks[n + QK_LOOKAHEAD]))
            m_prev = m_sc[:, lanes(c)]
            mt = jnp.max(st, axis=0, keepdims=True)
            p = jnp.exp2(st - m_prev)
            m_new = jnp.maximum(m_prev, mt)
            alpha = jnp.exp2(m_prev - m_new)
            gap_sc[:, lanes(c)] = jnp.maximum(gap_sc[:, lanes(c)], mt - m_prev)
            l_sc[:, lanes(c)] = (l_sc[:, lanes(c)] + jnp.sum(p, axis=0, keepdims=True)) * alpha
            upd = _dot(vt_ref[0, 0, j, :, 0:nk], p.astype(_BF16))
            acc_sc[c] = (acc_sc[c] + upd) * alpha
            m_sc[:, lanes(c)] = m_new

    n_full = 2 * i

    def two_pass_sweep():
        @pl.when(i > 0)
        def _():
            qk(0, 0)
            qk(1, 1)
            softmax_pv(0, 0)

            def pair(u, carry):
                j = 2 * u + 2
                qk(j, 0)
                softmax_pv(j - 1, 1)
                qk(j + 1, 1)
                softmax_pv(j, 0)
                return carry

            lax.fori_loop(0, i - 1, pair, 0)
            qk(n_full, 0, diag=0)
            softmax_pv(n_full - 1, 1)
            qk(n_full + 1, 1, diag=1)
            softmax_pv(n_full, 0, diag=0)
            softmax_pv(n_full + 1, 1, diag=1)

        @pl.when(i == 0)
        def _():
            qk(0, 0, diag=0)
            qk(1, 1, diag=1)
            softmax_pv(0, 0, diag=0)
            softmax_pv(1, 1, diag=1)

    gap_sc[...] = jnp.zeros(gap_sc.shape, _F32)

    @pl.when(i > 0)
    def _():
        qk(0, 0)
        softmax_pv(0, 0)
        stream(1)

        def pair(u, carry):
            j = 2 * u + 2
            stream(j)
            stream(j + 1)
            return carry

        lax.fori_loop(0, i - 1, pair, 0)
        stream(n_full, diag=0)
        stream(n_full + 1, diag=1)

    @pl.when(i == 0)
    def _():
        qk(0, 0, diag=0)
        softmax_pv(0, 0, diag=0)
        stream(1, diag=1)

    @pl.when(jnp.max(gap_sc[...]) > ONE_PASS_MAX_GAP)
    def _():
        m_sc[...] = jnp.full(m_sc.shape, NEG_INF, _F32)
        l_sc[...] = jnp.zeros(l_sc.shape, _F32)
        acc_sc[...] = jnp.zeros(acc_sc.shape, _F32)
        two_pass_sweep()

    lam_v = lam_ref[...]
    lam = (jnp.exp(jnp.sum(lam_v[0:1] * lam_v[1:2], axis=-1, keepdims=True))
           - jnp.exp(jnp.sum(lam_v[2:3] * lam_v[3:4], axis=-1, keepdims=True)) + lam_init)
    half = nblk // 2
    for c in range(half):
        ot = (acc_sc[c] / l_sc[:, lanes(c)]
              - lam * (acc_sc[c + half] / l_sc[:, lanes(c + half)]))
        ot = ot * lax.rsqrt(jnp.mean(ot * ot, axis=0, keepdims=True) + NORM_EPS)
        o_ref[lanes(c), :] = (ot.T * subg_ref[...] * (1.0 - lam_init)).astype(_BF16)


def _attn(layer, lam_vecs, q, k, vt, subg, batch, seq, lam_init):
    t = ATTN_TQ
    tk = t // 2
    nq = seq // t
    assert vt.shape[2:] == (seq // tk, DIFF_VDIM, tk)
    nblk = 2 * t // MXU_COLS
    stat = pltpu.VMEM((1, 2 * t), _F32)
    return pl.pallas_call(
        functools.partial(_attn_kernel, lam_init=lam_init),
        grid=(batch, DIFF_HEADS, nq),
        in_specs=[
            _layer_spec((4, DIFF_QK_DIM), layer),
            pl.BlockSpec((t, LANES), lambda b, h, i: (b * nq + i, h)),
            pl.BlockSpec((seq, LANES), lambda b, h, i: (b, h)),
            pl.BlockSpec((1, 1, seq // tk, DIFF_VDIM, tk), lambda b, h, i: (b, h, 0, 0, 0)),
            _layer_spec((1, DIFF_VDIM), layer),
        ],
        out_specs=pl.BlockSpec((t, LANES), lambda b, h, i: (b * nq + i, h)),
        out_shape=jax.ShapeDtypeStruct(q.shape, _BF16),
        scratch_shapes=[
            pltpu.VMEM((nblk, LANES, MXU_COLS), _BF16),
            pltpu.VMEM((nblk, tk, MXU_COLS), _F32),
            pltpu.VMEM((nblk, tk, MXU_COLS), _F32),
            stat, stat, stat, stat, stat,
            pltpu.VMEM((nblk, DIFF_VDIM, MXU_COLS), _F32),
        ],
        compiler_params=pltpu.CompilerParams(
            dimension_semantics=("arbitrary", "arbitrary", "arbitrary"),
            vmem_limit_bytes=VMEM_LIMIT_BYTES),
        name="diff_attn",
    )(lam_vecs, q, k, vt, subg)


def _post_kernel(ya_ref, yb_ref, x_ref, wo_ref, gpm_ref, gpf_ref, wgu_ref, wd_ref,
                 gpo_ref, o_ref, act_sc):
    tm = x_ref.shape[0]
    sub_tiles = [slice(r, r + POST_SUB_ROWS) for r in range(0, tm, POST_SUB_ROWS)]
    x1s, hs = [], []
    for rows in sub_tiles:
        mix = (_dot(ya_ref[rows, :], wo_ref[0:GMLP_WIDTH, :])
               + _dot(yb_ref[rows, :], wo_ref[GMLP_WIDTH:GMLP_WIDTH + DIFF_WIDTH, :]))
        x1 = x_ref[rows, :] + _rms(mix, gpm_ref[...])
        x1s.append(x1)
        hs.append(_rms(x1, gpf_ref[...]).astype(_BF16))

    for rows, x1, h in zip(sub_tiles, x1s, hs):
        for lo in range(0, FFN_HIDDEN, FFN_CHUNK):
            gate = _dot(h, wgu_ref[:, lo:lo + FFN_CHUNK])
            up = _dot(h, wgu_ref[:, FFN_HIDDEN + lo:FFN_HIDDEN + lo + FFN_CHUNK])
            act_sc[rows, lo:lo + FFN_CHUNK] = (gate * jax.nn.sigmoid(gate) * up).astype(_BF16)
        y = _dot(act_sc[rows, :], wd_ref[...])
        o_ref[rows, :] = x1 + _rms(y, gpo_ref[...])


def _post(layer, ya, yb, x, wo, gpm, gpf, wgu, wd, gpo):
    n, d = x.shape
    tm = ROW_TILE
    row = lambda i: (i, 0)
    par = functools.partial(_layer_spec, layer=layer)
    once = dict(pipeline_mode=pl.Buffered(1))
    return pl.pallas_call(
        _post_kernel,
        grid=(n // tm,),
        in_specs=[
            pl.BlockSpec((tm, GMLP_WIDTH), row),
            pl.BlockSpec((tm, DIFF_WIDTH), row),
            pl.BlockSpec((tm, d), row),
            par((GMLP_WIDTH + DIFF_WIDTH, d), **once),
            par((1, d)),
            par((1, d)),
            par((d, 2 * FFN_HIDDEN), **once),
            par((FFN_HIDDEN, d), **once),
            par((1, d)),
        ],
        out_specs=pl.BlockSpec((tm, d), row),
        out_shape=jax.ShapeDtypeStruct((n, d), _F32),
        scratch_shapes=[pltpu.VMEM((tm, FFN_HIDDEN), _BF16)],
        compiler_params=pltpu.CompilerParams(
            dimension_semantics=("arbitrary",), vmem_limit_bytes=VMEM_LIMIT_BYTES),
        name="post",
    )(ya, yb, x, wo, gpm, gpf, wgu, wd, gpo)


def _rope_tables(seq):
    half = DIFF_QK_DIM // 2
    inv = 1.0 / (ROPE_THETA ** (jnp.arange(0, DIFF_QK_DIM, 2, dtype=_F32) / DIFF_QK_DIM))
    ang = jnp.arange(seq, dtype=_F32)[:, None] * inv[None, :]
    cos, sin = jnp.cos(ang), jnp.sin(ang)
    zeros = jnp.zeros_like(sin)
    reps = LANES // DIFF_QK_DIM
    cos_t = jnp.tile(jnp.concatenate([cos, cos], axis=1), (1, reps))
    sa_t = jnp.tile(jnp.concatenate([-sin, zeros], axis=1), (1, reps))
    sb_t = jnp.tile(jnp.concatenate([zeros, sin], axis=1), (1, reps))
    del half
    return cos_t, sa_t, sb_t


def kernel(x, pre_mix_g, w_in, gmlp_ln_g, gmlp_ln_b, gmlp_ws, gmlp_b, lambda_q1, lambda_k1,
           lambda_q2, lambda_k2, subln_g, w_out, post_mix_g, pre_ffn_g, w_gate_up, w_down,
           post_ffn_g):
    batch, seq, d = x.shape
    depth = w_in.shape[0]
    assert d == D_MODEL and seq % ROW_TILE == 0 and seq % ATTN_TQ == 0
    assert FFN_HIDDEN % FFN_CHUNK == 0 and ATTN_TQ == 2 * ROW_TILE
    cos_t, sa_t, sb_t = _rope_tables(seq)
    xf = x.reshape(batch * seq, d)
    rows = lambda a: a[:, None, :]
    bias = jnp.broadcast_to(gmlp_b[..., None], gmlp_b.shape + (GMLP_GROUP_DIM,))
    lam_vecs = jnp.stack([lambda_q1, lambda_k1, lambda_q2, lambda_k2], axis=1)
    w_in_b, w_out_b = w_in.astype(_BF16), w_out.astype(_BF16)
    w_gu_b, w_down_b = w_gate_up.astype(_BF16), w_down.astype(_BF16)
    for l in range(depth):
        lam_init = 0.8 - 0.6 * math.exp(-0.3 * l)
        ya, q, k, vt = _inproj(l, xf, rows(pre_mix_g), w_in_b, rows(gmlp_ln_g), rows(gmlp_ln_b),
                               gmlp_ws, bias, cos_t, sa_t, sb_t, seq)
        yb = _attn(l, lam_vecs, q, k, vt, rows(subln_g), batch, seq, lam_init)
        xf = _post(l, ya, yb, xf, w_out_b, rows(post_mix_g), rows(pre_ffn_g), w_gu_b, w_down_b,
                   rows(post_ffn_g))
    return xf.reshape(batch, seq, d)
```

```python
import functools
import math

import jax
import jax.numpy as jnp
from jax import lax
from jax.experimental import pallas as pl
from jax.experimental.pallas import tpu as pltpu

D_MODEL = 1024
CHUNK = 64
GMLP_WIDTH = 512
GMLP_GROUPS = 4
GMLP_GROUP_DIM = 128
GMLP_BLOCK = 128
DIFF_WIDTH = 512
DIFF_HEADS = 4
DIFF_VDIM = 128
DIFF_QK_DIM = 64
ROPE_THETA = 10000.0
FFN_HIDDEN = 2816
IN_WIDTH = 2560
NORM_EPS = 1e-6
NEG_INF = -1e30
ONE_PASS_MAX_STATE = 2.0 ** 100
QK_LOOKAHEAD = 3

LANES = 128
MXU_COLS = 256
BF16_SUBLANES = 16
VT_ROWS = DIFF_VDIM + BF16_SUBLANES
VMEM_LIMIT_BYTES = 56 * 1024 * 1024

ROW_TILE = 512
ATTN_TQ = 1024
FFN_CHUNK = 256
POST_SUB_ROWS = 256

_BF16 = jnp.bfloat16
_F32 = jnp.float32


def _dot(a, b):
    return jnp.dot(a, b, preferred_element_type=_F32)


def _rms(x, g):
    return x * lax.rsqrt(jnp.mean(x * x, axis=-1, keepdims=True) + NORM_EPS) * g


def _gelu(x):
    return 0.5 * x * (1.0 + lax.erf(x * math.sqrt(0.5)))


def _inproj_kernel(x_ref, g_ref, w_ref, lng_ref, lnb_ref, ws_ref, bias_ref,
                   cos_ref, sa_ref, sb_ref, ya_ref, q_ref, k_ref, vt_ref):
    tm = x_ref.shape[0]
    h = _rms(x_ref[...], g_ref[...]).astype(_BF16)

    vg = _gelu(_dot(h, w_ref[:, GMLP_WIDTH:2 * GMLP_WIDTH]))
    mu = jnp.mean(vg, axis=-1, keepdims=True)
    vc = vg - mu
    var = jnp.mean(vc * vc, axis=-1, keepdims=True)
    vln = (vc * lax.rsqrt(var + NORM_EPS) * lng_ref[...] + lnb_ref[...]).astype(_BF16)
    u = _gelu(_dot(h, w_ref[:, 0:GMLP_WIDTH]))

    cos = cos_ref[...]
    sa = sa_ref[...]
    sb = sb_ref[...]

    def rope(z, scale):
        out = []
        for hb in range(z.shape[1] // LANES):
            blk = z[:, hb * LANES:(hb + 1) * LANES]
            rot = (blk * cos + pltpu.roll(blk, LANES - DIFF_QK_DIM // 2, 1) * sa
                   + pltpu.roll(blk, DIFF_QK_DIM // 2, 1) * sb)
            out.append((rot * scale).astype(_BF16))
        return out

    o_q = 2 * GMLP_WIDTH
    o_k = o_q + DIFF_WIDTH
    o_v = o_k + DIFF_WIDTH
    for hb, blk in enumerate(rope(_dot(h, w_ref[:, o_q:o_k]), DIFF_QK_DIM ** -0.5 * math.log2(math.e))):
        q_ref[:, hb * LANES:(hb + 1) * LANES] = blk
    for hb, blk in enumerate(rope(_dot(h, w_ref[:, o_k:o_v]), 1.0)):
        k_ref[:, hb * LANES:(hb + 1) * LANES] = blk
    vals = _dot(h, w_ref[:, o_v:IN_WIDTH])
    for hb in range(DIFF_HEADS):
        vt_ref[0, hb, 0, 0:DIFF_VDIM, :] = (
            vals[:, hb * DIFF_VDIM:(hb + 1) * DIFF_VDIM].T.astype(_BF16))
        vt_ref[0, hb, 0, DIFF_VDIM:VT_ROWS, :] = jnp.ones((VT_ROWS - DIFF_VDIM, tm), _BF16)

    t_chunk = lax.broadcasted_iota(jnp.int32, (GMLP_BLOCK, GMLP_BLOCK), 0) // CHUNK
    s_chunk = lax.broadcasted_iota(jnp.int32, (GMLP_BLOCK, GMLP_BLOCK), 1) // CHUNK
    causal = s_chunk <= t_chunk
    for g in range(GMLP_GROUPS):
        w_g = jnp.where(causal, ws_ref[g], 0.0).astype(_BF16)
        cols = slice(g * GMLP_GROUP_DIM, (g + 1) * GMLP_GROUP_DIM)
        for r in range(tm // GMLP_BLOCK):
            rows = slice(r * GMLP_BLOCK, (r + 1) * GMLP_BLOCK)
            mixed = _dot(w_g, vln[rows, cols]) + bias_ref[g]
            ya_ref[rows, cols] = (u[rows, cols] * mixed).astype(_BF16)


def _layer_spec(shape, layer, **kwargs):
    index = (layer,) + (0,) * len(shape)
    return pl.BlockSpec((None,) + tuple(shape), lambda *_: index, **kwargs)


def _inproj(layer, x, g, w, lng, lnb, ws, bias, cos, sa, sb, seq):
    n, d = x.shape
    tm = ROW_TILE
    seq_tiles = seq // tm
    row = lambda i: (i, 0)
    par = functools.partial(_layer_spec, layer=layer)
    pos = lambda i: (i % seq_tiles, 0)
    out_sds = jax.ShapeDtypeStruct((n, DIFF_WIDTH), _BF16)
    vt_sds = jax.ShapeDtypeStruct((n // seq, DIFF_HEADS, seq_tiles, VT_ROWS, tm), _BF16)
    vt_spec = pl.BlockSpec((1, DIFF_HEADS, 1, VT_ROWS, tm),
                           lambda i: (i // seq_tiles, 0, i % seq_tiles, 0, 0))
    return pl.pallas_call(
        _inproj_kernel,
        grid=(n // tm,),
        in_specs=[
            pl.BlockSpec((tm, d), row),
            par((1, d)),
            par((d, IN_WIDTH)),
            par((1, GMLP_WIDTH)),
            par((1, GMLP_WIDTH)),
            par((GMLP_GROUPS, GMLP_BLOCK, GMLP_BLOCK)),
            par((GMLP_GROUPS, GMLP_BLOCK, GMLP_GROUP_DIM)),
            pl.BlockSpec((tm, LANES), pos),
            pl.BlockSpec((tm, LANES), pos),
            pl.BlockSpec((tm, LANES), pos),
        ],
        out_specs=[pl.BlockSpec((tm, DIFF_WIDTH), row)] * 3 + [vt_spec],
        out_shape=[out_sds] * 3 + [vt_sds],
        compiler_params=pltpu.CompilerParams(
            dimension_semantics=("arbitrary",), vmem_limit_bytes=VMEM_LIMIT_BYTES),
        name="inproj",
    )(x, g, w, lng, lnb, ws, bias, cos, sa, sb)


def _attn_kernel(lam_ref, q_ref, k_ref, vt_ref, subg_ref, o_ref,
                 qqt_sc, s0_sc, s1_sc, mt0_sc, mt1_sc, m_sc, l_sc, acc_sc, *, lam_init):
    t = q_ref.shape[0]
    nblk, tk, cw = s0_sc.shape
    i = pl.program_id(2)
    s_sc, mt_sc = (s0_sc, s1_sc), (mt0_sc, mt1_sc)
    lanes = lambda c: slice(c * cw, (c + 1) * cw)

    qt = q_ref[...].astype(_F32).T
    sub = lax.broadcasted_iota(jnp.int32, (LANES, cw), 0)
    for c in range(nblk):
        branch = sub < DIFF_QK_DIM if c < nblk // 2 else sub >= DIFF_QK_DIM
        qqt_sc[c] = jnp.where(branch, qt[:, lanes(c % (nblk // 2))], 0.0).astype(_BF16)
    m_sc[...] = jnp.full(m_sc.shape, NEG_INF, _F32)
    l_sc[...] = jnp.zeros(l_sc.shape, _F32)
    acc_sc[...] = jnp.zeros(acc_sc.shape, _F32)

    def visible_keys(c, diag):
        if diag is None:
            return tk
        return min(max(c % (nblk // 2) * cw + cw - diag * tk, 0), tk)

    def qk(j, b, diag=None):
        start = pl.multiple_of(j * tk, tk)
        for c in range(nblk):
            nk = visible_keys(c, diag)
            if nk == 0:
                continue
            st = _dot(k_ref[pl.ds(start, nk), :], qqt_sc[c])
            if diag is not None:
                kc = (lax.broadcasted_iota(jnp.int32, st.shape, 0) + diag * tk) // CHUNK
                qc = (lax.broadcasted_iota(jnp.int32, st.shape, 1) + c % (nblk // 2) * cw) // CHUNK
                st = jnp.where(kc <= qc, st, NEG_INF)
            s_sc[b][c, 0:nk, :] = st
            mt_sc[b][:, lanes(c)] = jnp.max(st, axis=0, keepdims=True)

    def softmax_pv(j, b, diag=None):
        for c in range(nblk):
            nk = visible_keys(c, diag)
            if nk == 0:
                continue
            m_prev = m_sc[:, lanes(c)]
            m_new = jnp.maximum(m_prev, mt_sc[b][:, lanes(c)])
            alpha = jnp.exp2(m_prev - m_new)
            p = jnp.exp2(s_sc[b][c, 0:nk, :] - m_new)
            m_sc[:, lanes(c)] = m_new
            upd = _dot(vt_ref[0, 0, j, :, 0:nk], p.astype(_BF16))
            l_sc[:, lanes(c)] = alpha * l_sc[:, lanes(c)] + upd[DIFF_VDIM:DIFF_VDIM + 1]
            acc_sc[c] = alpha * acc_sc[c] + upd[0:DIFF_VDIM]

    def stream(*tiles):
        def scores(j, diag, c):
            nk = visible_keys(c, diag)
            start = pl.multiple_of(j * tk, tk)
            st = _dot(k_ref[pl.ds(start, nk), :], qqt_sc[c])
            if diag is not None:
                kc = (lax.broadcasted_iota(jnp.int32, st.shape, 0) + diag * tk) // CHUNK
                qc = (lax.broadcasted_iota(jnp.int32, st.shape, 1) + c % (nblk // 2) * cw) // CHUNK
                st = jnp.where(kc <= qc, st, NEG_INF)
            return st

        items = [(j, diag, c) for j, diag in tiles for c in range(nblk)
                 if visible_keys(c, diag) > 0]
        ahead = [scores(*item) for item in items[:QK_LOOKAHEAD]]
        for n, (j, diag, c) in enumerate(items):
            nk = visible_keys(c, diag)
            st = ahead.pop(0)
            if n + QK_LOOKAHEAD < len(items):
                ahead.append(scores(*items[n + QK_LOOKAHEAD]))
            p = jnp.exp2(st - m_sc[:, lanes(c)])
            upd = _dot(vt_ref[0, 0, j, :, 0:nk], p.astype(_BF16))
            l_sc[:, lanes(c)] = l_sc[:, lanes(c)] + upd[DIFF_VDIM:DIFF_VDIM + 1]
            acc_sc[c] = acc_sc[c] + upd[0:DIFF_VDIM]

    n_full = 2 * i

    def two_pass_sweep():
        @pl.when(i > 0)
        def _():
            qk(0, 0)
            qk(1, 1)
            softmax_pv(0, 0)

            def pair(u, carry):
                j = 2 * u + 2
                qk(j, 0)
                softmax_pv(j - 1, 1)
                qk(j + 1, 1)
                softmax_pv(j, 0)
                return carry

            lax.fori_loop(0, i - 1, pair, 0)
            qk(n_full, 0, diag=0)
            softmax_pv(n_full - 1, 1)
            qk(n_full + 1, 1, diag=1)
            softmax_pv(n_full, 0, diag=0)
            softmax_pv(n_full + 1, 1, diag=1)

        @pl.when(i == 0)
        def _():
            qk(0, 0, diag=0)
            qk(1, 1, diag=1)
            softmax_pv(0, 0, diag=0)
            softmax_pv(1, 1, diag=1)

    @pl.when(i > 0)
    def _():
        qk(0, 0)
        softmax_pv(0, 0)
        stream((1, None))

        def pair(u, carry):
            j = 2 * u + 2
            stream((j, None), (j + 1, None))
            return carry

        lax.fori_loop(0, i - 1, pair, 0)
        stream((n_full, 0), (n_full + 1, 1))

    @pl.when(i == 0)
    def _():
        qk(0, 0, diag=0)
        softmax_pv(0, 0, diag=0)
        stream((1, 1))

    unsafe = (jnp.sum(jnp.where(jnp.abs(acc_sc[...]) < ONE_PASS_MAX_STATE, 0.0, 1.0))
              + jnp.sum(jnp.where(l_sc[...] < ONE_PASS_MAX_STATE, 0.0, 1.0)))

    @pl.when(unsafe > 0.0)
    def _():
        m_sc[...] = jnp.full(m_sc.shape, NEG_INF, _F32)
        l_sc[...] = jnp.zeros(l_sc.shape, _F32)
        acc_sc[...] = jnp.zeros(acc_sc.shape, _F32)
        two_pass_sweep()

    lam_v = lam_ref[...]
    lam = (jnp.exp(jnp.sum(lam_v[0:1] * lam_v[1:2], axis=-1, keepdims=True))
           - jnp.exp(jnp.sum(lam_v[2:3] * lam_v[3:4], axis=-1, keepdims=True)) + lam_init)
    half = nblk // 2
    for c in range(half):
        ot = (acc_sc[c] / l_sc[:, lanes(c)]
              - lam * (acc_sc[c + half] / l_sc[:, lanes(c + half)]))
        ot = ot * lax.rsqrt(jnp.mean(ot * ot, axis=0, keepdims=True) + NORM_EPS)
        o_ref[lanes(c), :] = (ot.T * subg_ref[...] * (1.0 - lam_init)).astype(_BF16)


def _attn(layer, lam_vecs, q, k, vt, subg, batch, seq, lam_init):
    t = ATTN_TQ
    tk = t // 2
    nq = seq // t
    assert vt.shape[2:] == (seq // tk, VT_ROWS, tk)
    nblk = 2 * t // MXU_COLS
    stat = pltpu.VMEM((1, 2 * t), _F32)
    return pl.pallas_call(
        functools.partial(_attn_kernel, lam_init=lam_init),
        grid=(batch, DIFF_HEADS, nq),
        in_specs=[
            _layer_spec((4, DIFF_QK_DIM), layer),
            pl.BlockSpec((t, LANES), lambda b, h, i: (b * nq + i, h)),
            pl.BlockSpec((seq, LANES), lambda b, h, i: (b, h)),
            pl.BlockSpec((1, 1, seq // tk, VT_ROWS, tk), lambda b, h, i: (b, h, 0, 0, 0)),
            _layer_spec((1, DIFF_VDIM), layer),
        ],
        out_specs=pl.BlockSpec((t, LANES), lambda b, h, i: (b * nq + i, h)),
        out_shape=jax.ShapeDtypeStruct(q.shape, _BF16),
        scratch_shapes=[
            pltpu.VMEM((nblk, LANES, MXU_COLS), _BF16),
            pltpu.VMEM((nblk, tk, MXU_COLS), _F32),
            pltpu.VMEM((nblk, tk, MXU_COLS), _F32),
            stat, stat, stat, stat,
            pltpu.VMEM((nblk, DIFF_VDIM, MXU_COLS), _F32),
        ],
        compiler_params=pltpu.CompilerParams(
            dimension_semantics=("arbitrary", "arbitrary", "arbitrary"),
            vmem_limit_bytes=VMEM_LIMIT_BYTES),
        name="diff_attn",
    )(lam_vecs, q, k, vt, subg)


def _post_kernel(ya_ref, yb_ref, x_ref, wo_ref, gpm_ref, gpf_ref, wgu_ref, wd_ref,
                 gpo_ref, o_ref, act_sc):
    tm = x_ref.shape[0]
    sub_tiles = [slice(r, r + POST_SUB_ROWS) for r in range(0, tm, POST_SUB_ROWS)]
    x1s, hs = [], []
    for rows in sub_tiles:
        mix = (_dot(ya_ref[rows, :], wo_ref[0:GMLP_WIDTH, :])
               + _dot(yb_ref[rows, :], wo_ref[GMLP_WIDTH:GMLP_WIDTH + DIFF_WIDTH, :]))
        x1 = x_ref[rows, :] + _rms(mix, gpm_ref[...])
        x1s.append(x1)
        hs.append(_rms(x1, gpf_ref[...]).astype(_BF16))

    for rows, x1, h in zip(sub_tiles, x1s, hs):
        for lo in range(0, FFN_HIDDEN, FFN_CHUNK):
            gate = _dot(h, wgu_ref[:, lo:lo + FFN_CHUNK])
            up = _dot(h, wgu_ref[:, FFN_HIDDEN + lo:FFN_HIDDEN + lo + FFN_CHUNK])
            act_sc[rows, lo:lo + FFN_CHUNK] = (gate * jax.nn.sigmoid(gate) * up).astype(_BF16)
        y = _dot(act_sc[rows, :], wd_ref[...])
        o_ref[rows, :] = x1 + _rms(y, gpo_ref[...])


def _post(layer, ya, yb, x, wo, gpm, gpf, wgu, wd, gpo):
    n, d = x.shape
    tm = ROW_TILE
    row = lambda i: (i, 0)
    par = functools.partial(_layer_spec, layer=layer)
    once = dict(pipeline_mode=pl.Buffered(1))
    return pl.pallas_call(
        _post_kernel,
        grid=(n // tm,),
        in_specs=[
            pl.BlockSpec((tm, GMLP_WIDTH), row),
            pl.BlockSpec((tm, DIFF_WIDTH), row),
            pl.BlockSpec((tm, d), row),
            par((GMLP_WIDTH + DIFF_WIDTH, d), **once),
            par((1, d)),
            par((1, d)),
            par((d, 2 * FFN_HIDDEN), **once),
            par((FFN_HIDDEN, d), **once),
            par((1, d)),
        ],
        out_specs=pl.BlockSpec((tm, d), row),
        out_shape=jax.ShapeDtypeStruct((n, d), _F32),
        scratch_shapes=[pltpu.VMEM((tm, FFN_HIDDEN), _BF16)],
        compiler_params=pltpu.CompilerParams(
            dimension_semantics=("arbitrary",), vmem_limit_bytes=VMEM_LIMIT_BYTES),
        name="post",
    )(ya, yb, x, wo, gpm, gpf, wgu, wd, gpo)


def _rope_tables(seq):
    half = DIFF_QK_DIM // 2
    inv = 1.0 / (ROPE_THETA ** (jnp.arange(0, DIFF_QK_DIM, 2, dtype=_F32) / DIFF_QK_DIM))
    ang = jnp.arange(seq, dtype=_F32)[:, None] * inv[None, :]
    cos, sin = jnp.cos(ang), jnp.sin(ang)
    zeros = jnp.zeros_like(sin)
    reps = LANES // DIFF_QK_DIM
    cos_t = jnp.tile(jnp.concatenate([cos, cos], axis=1), (1, reps))
    sa_t = jnp.tile(jnp.concatenate([-sin, zeros], axis=1), (1, reps))
    sb_t = jnp.tile(jnp.concatenate([zeros, sin], axis=1), (1, reps))
    del half
    return cos_t, sa_t, sb_t


def kernel(x, pre_mix_g, w_in, gmlp_ln_g, gmlp_ln_b, gmlp_ws, gmlp_b, lambda_q1, lambda_k1,
           lambda_q2, lambda_k2, subln_g, w_out, post_mix_g, pre_ffn_g, w_gate_up, w_down,
           post_ffn_g):
    batch, seq, d = x.shape
    depth = w_in.shape[0]
    assert d == D_MODEL and seq % ROW_TILE == 0 and seq % ATTN_TQ == 0
    assert FFN_HIDDEN % FFN_CHUNK == 0 and ATTN_TQ == 2 * ROW_TILE
    cos_t, sa_t, sb_t = _rope_tables(seq)
    xf = x.reshape(batch * seq, d)
    rows = lambda a: a[:, None, :]
    bias = jnp.broadcast_to(gmlp_b[..., None], gmlp_b.shape + (GMLP_GROUP_DIM,))
    lam_vecs = jnp.stack([lambda_q1, lambda_k1, lambda_q2, lambda_k2], axis=1)
    w_in_b, w_out_b = w_in.astype(_BF16), w_out.astype(_BF16)
    w_gu_b, w_down_b = w_gate_up.astype(_BF16), w_down.astype(_BF16)
    for l in range(depth):
        lam_init = 0.8 - 0.6 * math.exp(-0.3 * l)
        ya, q, k, vt = _inproj(l, xf, rows(pre_mix_g), w_in_b, rows(gmlp_ln_g), rows(gmlp_ln_b),
                               gmlp_ws, bias, cos_t, sa_t, sb_t, seq)
        yb = _attn(l, lam_vecs, q, k, vt, rows(subln_g), batch, seq, lam_init)
        xf = _post(l, ya, yb, xf, w_out_b, rows(post_mix_g), rows(pre_ffn_g), w_gu_b, w_down_b,
                   rows(post_ffn_g))
    return xf.reshape(batch, seq, d)
```

```python
import functools
import math

import jax
import jax.numpy as jnp
from jax import lax
from jax.experimental import pallas as pl
from jax.experimental.pallas import tpu as pltpu

D_MODEL = 1024
CHUNK = 64
GMLP_WIDTH = 512
GMLP_GROUPS = 4
GMLP_GROUP_DIM = 128
GMLP_BLOCK = 128
DIFF_WIDTH = 512
DIFF_HEADS = 4
DIFF_VDIM = 128
DIFF_QK_DIM = 64
ROPE_THETA = 10000.0
FFN_HIDDEN = 2816
IN_WIDTH = 2560
NORM_EPS = 1e-6
NEG_INF = -1e30
ONE_PASS_MAX_STATE = 2.0 ** 100
QK_LOOKAHEAD = 3

LANES = 128
MXU_COLS = 256
BF16_SUBLANES = 16
VT_ROWS = DIFF_VDIM + BF16_SUBLANES
VMEM_LIMIT_BYTES = 56 * 1024 * 1024

ROW_TILE = 512
ATTN_TQ = 1024
FFN_CHUNK = 256
POST_SUB_ROWS = 256

_BF16 = jnp.bfloat16
_F32 = jnp.float32


def _dot(a, b):
    return jnp.dot(a, b, preferred_element_type=_F32)


def _rms(x, g):
    return x * lax.rsqrt(jnp.mean(x * x, axis=-1, keepdims=True) + NORM_EPS) * g


def _gelu(x):
    return 0.5 * x * (1.0 + lax.erf(x * math.sqrt(0.5)))


def _inproj_kernel(x_ref, g_ref, w_ref, lng_ref, lnb_ref, ws_ref, bias_ref,
                   cos_ref, sa_ref, sb_ref, ya_ref, q_ref, k_ref, vt_ref):
    tm = x_ref.shape[0]
    h = _rms(x_ref[...], g_ref[...]).astype(_BF16)

    vg = _gelu(_dot(h, w_ref[:, GMLP_WIDTH:2 * GMLP_WIDTH]))
    mu = jnp.mean(vg, axis=-1, keepdims=True)
    vc = vg - mu
    var = jnp.mean(vc * vc, axis=-1, keepdims=True)
    vln = (vc * lax.rsqrt(var + NORM_EPS) * lng_ref[...] + lnb_ref[...]).astype(_BF16)
    u = _gelu(_dot(h, w_ref[:, 0:GMLP_WIDTH]))

    cos = cos_ref[...]
    sa = sa_ref[...]
    sb = sb_ref[...]

    def rope(z, scale):
        out = []
        for hb in range(z.shape[1] // LANES):
            blk = z[:, hb * LANES:(hb + 1) * LANES]
            rot = (blk * cos + pltpu.roll(blk, LANES - DIFF_QK_DIM // 2, 1) * sa
                   + pltpu.roll(blk, DIFF_QK_DIM // 2, 1) * sb)
            out.append((rot * scale).astype(_BF16))
        return out

    o_q = 2 * GMLP_WIDTH
    o_k = o_q + DIFF_WIDTH
    o_v = o_k + DIFF_WIDTH
    for hb, blk in enumerate(rope(_dot(h, w_ref[:, o_q:o_k]), DIFF_QK_DIM ** -0.5 * math.log2(math.e))):
        q_ref[:, hb * LANES:(hb + 1) * LANES] = blk
    for hb, blk in enumerate(rope(_dot(h, w_ref[:, o_k:o_v]), 1.0)):
        k_ref[:, hb * LANES:(hb + 1) * LANES] = blk
    vals = _dot(h, w_ref[:, o_v:IN_WIDTH])
    for hb in range(DIFF_HEADS):
        vt_ref[0, hb, 0, 0:DIFF_VDIM, :] = (
            vals[:, hb * DIFF_VDIM:(hb + 1) * DIFF_VDIM].T.astype(_BF16))
        vt_ref[0, hb, 0, DIFF_VDIM:VT_ROWS, :] = jnp.ones((VT_ROWS - DIFF_VDIM, tm), _BF16)

    t_chunk = lax.broadcasted_iota(jnp.int32, (GMLP_BLOCK, GMLP_BLOCK), 0) // CHUNK
    s_chunk = lax.broadcasted_iota(jnp.int32, (GMLP_BLOCK, GMLP_BLOCK), 1) // CHUNK
    causal = s_chunk <= t_chunk
    for g in range(GMLP_GROUPS):
        w_g = jnp.where(causal, ws_ref[g], 0.0).astype(_BF16)
        cols = slice(g * GMLP_GROUP_DIM, (g + 1) * GMLP_GROUP_DIM)
        for r in range(tm // GMLP_BLOCK):
            rows = slice(r * GMLP_BLOCK, (r + 1) * GMLP_BLOCK)
            mixed = _dot(w_g, vln[rows, cols]) + bias_ref[g]
            ya_ref[rows, cols] = (u[rows, cols] * mixed).astype(_BF16)


def _layer_spec(shape, layer, **kwargs):
    index = (layer,) + (0,) * len(shape)
    return pl.BlockSpec((None,) + tuple(shape), lambda *_: index, **kwargs)


def _inproj(layer, x, g, w, lng, lnb, ws, bias, cos, sa, sb, seq):
    n, d = x.shape
    tm = ROW_TILE
    seq_tiles = seq // tm
    row = lambda i: (i, 0)
    par = functools.partial(_layer_spec, layer=layer)
    pos = lambda i: (i % seq_tiles, 0)
    out_sds = jax.ShapeDtypeStruct((n, DIFF_WIDTH), _BF16)
    vt_sds = jax.ShapeDtypeStruct((n // seq, DIFF_HEADS, seq_tiles, VT_ROWS, tm), _BF16)
    vt_spec = pl.BlockSpec((1, DIFF_HEADS, 1, VT_ROWS, tm),
                           lambda i: (i // seq_tiles, 0, i % seq_tiles, 0, 0))
    return pl.pallas_call(
        _inproj_kernel,
        grid=(n // tm,),
        in_specs=[
            pl.BlockSpec((tm, d), row),
            par((1, d)),
            par((d, IN_WIDTH)),
            par((1, GMLP_WIDTH)),
            par((1, GMLP_WIDTH)),
            par((GMLP_GROUPS, GMLP_BLOCK, GMLP_BLOCK)),
            par((GMLP_GROUPS, GMLP_BLOCK, GMLP_GROUP_DIM)),
            pl.BlockSpec((tm, LANES), pos),
            pl.BlockSpec((tm, LANES), pos),
            pl.BlockSpec((tm, LANES), pos),
        ],
        out_specs=[pl.BlockSpec((tm, DIFF_WIDTH), row)] * 3 + [vt_spec],
        out_shape=[out_sds] * 3 + [vt_sds],
        compiler_params=pltpu.CompilerParams(
            dimension_semantics=("arbitrary",), vmem_limit_bytes=VMEM_LIMIT_BYTES),
        name="inproj",
    )(x, g, w, lng, lnb, ws, bias, cos, sa, sb)


def _attn_kernel(lam_ref, q_ref, k_ref, vt_ref, subg_ref, o_ref,
                 qqt_sc, s0_sc, s1_sc, mt0_sc, mt1_sc, m_sc, l_sc, acc_sc, *, lam_init):
    t = q_ref.shape[0]
    nblk, tk, cw = s0_sc.shape
    i = pl.program_id(2)
    s_sc, mt_sc = (s0_sc, s1_sc), (mt0_sc, mt1_sc)
    lanes = lambda c: slice(c * cw, (c + 1) * cw)

    qt = q_ref[...].astype(_F32).T
    sub = lax.broadcasted_iota(jnp.int32, (LANES, cw), 0)
    for c in range(nblk):
        branch = sub < DIFF_QK_DIM if c < nblk // 2 else sub >= DIFF_QK_DIM
        qqt_sc[c] = jnp.where(branch, qt[:, lanes(c % (nblk // 2))], 0.0).astype(_BF16)

    def visible_keys(c, diag):
        if diag is None:
            return tk
        return min(max(c % (nblk // 2) * cw + cw - diag * tk, 0), tk)

    def qk(j, b, diag=None):
        start = pl.multiple_of(j * tk, tk)
        for c in range(nblk):
            nk = visible_keys(c, diag)
            if nk == 0:
                continue
            st = _dot(k_ref[pl.ds(start, nk), :], qqt_sc[c])
            if diag is not None:
                kc = (lax.broadcasted_iota(jnp.int32, st.shape, 0) + diag * tk) // CHUNK
                qc = (lax.broadcasted_iota(jnp.int32, st.shape, 1) + c % (nblk // 2) * cw) // CHUNK
                st = jnp.where(kc <= qc, st, NEG_INF)
            s_sc[b][c, 0:nk, :] = st
            mt_sc[b][:, lanes(c)] = jnp.max(st, axis=0, keepdims=True)

    def softmax_pv(j, b, diag=None, first=False):
        for c in range(nblk):
            nk = visible_keys(c, diag)
            if nk == 0:
                assert not first
                continue
            if first:
                m_new = mt_sc[b][:, lanes(c)]
            else:
                m_prev = m_sc[:, lanes(c)]
                m_new = jnp.maximum(m_prev, mt_sc[b][:, lanes(c)])
                alpha = jnp.exp2(m_prev - m_new)
            p = jnp.exp2(s_sc[b][c, 0:nk, :] - m_new)
            m_sc[:, lanes(c)] = m_new
            upd = _dot(vt_ref[0, 0, j, :, 0:nk], p.astype(_BF16))
            if first:
                l_sc[:, lanes(c)] = upd[DIFF_VDIM:DIFF_VDIM + 1]
                acc_sc[c] = upd[0:DIFF_VDIM]
            else:
                l_sc[:, lanes(c)] = alpha * l_sc[:, lanes(c)] + upd[DIFF_VDIM:DIFF_VDIM + 1]
                acc_sc[c] = alpha * acc_sc[c] + upd[0:DIFF_VDIM]

    def stream(*tiles):
        def scores(j, diag, c):
            nk = visible_keys(c, diag)
            start = pl.multiple_of(j * tk, tk)
            st = _dot(k_ref[pl.ds(start, nk), :], qqt_sc[c])
            if diag is not None:
                kc = (lax.broadcasted_iota(jnp.int32, st.shape, 0) + diag * tk) // CHUNK
                qc = (lax.broadcasted_iota(jnp.int32, st.shape, 1) + c % (nblk // 2) * cw) // CHUNK
                st = jnp.where(kc <= qc, st, NEG_INF)
            return st

        items = [(j, diag, c) for j, diag in tiles for c in range(nblk)
                 if visible_keys(c, diag) > 0]
        ahead = [scores(*item) for item in items[:QK_LOOKAHEAD]]
        for n, (j, diag, c) in enumerate(items):
            nk = visible_keys(c, diag)
            st = ahead.pop(0)
            if n + QK_LOOKAHEAD < len(items):
                ahead.append(scores(*items[n + QK_LOOKAHEAD]))
            p = jnp.exp2(st - m_sc[:, lanes(c)])
            upd = _dot(vt_ref[0, 0, j, :, 0:nk], p.astype(_BF16))
            l_sc[:, lanes(c)] = l_sc[:, lanes(c)] + upd[DIFF_VDIM:DIFF_VDIM + 1]
            acc_sc[c] = acc_sc[c] + upd[0:DIFF_VDIM]

    n_full = 2 * i

    def two_pass_sweep():
        @pl.when(i > 0)
        def _():
            qk(0, 0)
            qk(1, 1)
            softmax_pv(0, 0, first=True)

            def pair(u, carry):
                j = 2 * u + 2
                qk(j, 0)
                softmax_pv(j - 1, 1)
                qk(j + 1, 1)
                softmax_pv(j, 0)
                return carry

            lax.fori_loop(0, i - 1, pair, 0)
            qk(n_full, 0, diag=0)
            softmax_pv(n_full - 1, 1)
            qk(n_full + 1, 1, diag=1)
            softmax_pv(n_full, 0, diag=0)
            softmax_pv(n_full + 1, 1, diag=1)

        @pl.when(i == 0)
        def _():
            qk(0, 0, diag=0)
            qk(1, 1, diag=1)
            softmax_pv(0, 0, diag=0, first=True)
            softmax_pv(1, 1, diag=1)

    @pl.when(i > 0)
    def _():
        qk(0, 0)
        softmax_pv(0, 0, first=True)

        def pair(u, carry):
            j = 2 * u + 1
            stream((j, None), (j + 1, None))
            return carry

        lax.fori_loop(0, i - 1, pair, 0)
        stream((n_full - 1, None), (n_full, 0), (n_full + 1, 1))

    @pl.when(i == 0)
    def _():
        qk(0, 0, diag=0)
        softmax_pv(0, 0, diag=0, first=True)
        stream((1, 1))

    def finalize():
        lam_v = lam_ref[...]
        lam = (jnp.exp(jnp.sum(lam_v[0:1] * lam_v[1:2], axis=-1, keepdims=True))
               - jnp.exp(jnp.sum(lam_v[2:3] * lam_v[3:4], axis=-1, keepdims=True)) + lam_init)
        unsafe = jnp.sum(jnp.where(l_sc[...] < ONE_PASS_MAX_STATE, 0.0, 1.0))
        half = nblk // 2
        for c in range(half):
            ot = (acc_sc[c] / l_sc[:, lanes(c)]
                  - lam * (acc_sc[c + half] / l_sc[:, lanes(c + half)]))
            ms = jnp.mean(ot * ot, axis=0, keepdims=True)
            unsafe = unsafe + jnp.sum(jnp.where(ms < ONE_PASS_MAX_STATE, 0.0, 1.0))
            ot = ot * lax.rsqrt(ms + NORM_EPS)
            o_ref[lanes(c), :] = (ot.T * subg_ref[...] * (1.0 - lam_init)).astype(_BF16)
        return unsafe

    @pl.when(finalize() > 0.0)
    def _():
        two_pass_sweep()
        finalize()


def _attn(layer, lam_vecs, q, k, vt, subg, batch, seq, lam_init):
    t = ATTN_TQ
    tk = t // 2
    nq = seq // t
    assert vt.shape[2:] == (seq // tk, VT_ROWS, tk)
    nblk = 2 * t // MXU_COLS
    stat = pltpu.VMEM((1, 2 * t), _F32)
    return pl.pallas_call(
        functools.partial(_attn_kernel, lam_init=lam_init),
        grid=(batch, DIFF_HEADS, nq),
        in_specs=[
            _layer_spec((4, DIFF_QK_DIM), layer),
            pl.BlockSpec((t, LANES), lambda b, h, i: (b * nq + i, h)),
            pl.BlockSpec((seq, LANES), lambda b, h, i: (b, h)),
            pl.BlockSpec((1, 1, seq // tk, VT_ROWS, tk), lambda b, h, i: (b, h, 0, 0, 0)),
            _layer_spec((1, DIFF_VDIM), layer),
        ],
        out_specs=pl.BlockSpec((t, LANES), lambda b, h, i: (b * nq + i, h)),
        out_shape=jax.ShapeDtypeStruct(q.shape, _BF16),
        scratch_shapes=[
            pltpu.VMEM((nblk, LANES, MXU_COLS), _BF16),
            pltpu.VMEM((nblk, tk, MXU_COLS), _F32),
            pltpu.VMEM((nblk, tk, MXU_COLS), _F32),
            stat, stat, stat, stat,
            pltpu.VMEM((nblk, DIFF_VDIM, MXU_COLS), _F32),
        ],
        compiler_params=pltpu.CompilerParams(
            dimension_semantics=("arbitrary", "arbitrary", "arbitrary"),
            vmem_limit_bytes=VMEM_LIMIT_BYTES),
        name="diff_attn",
    )(lam_vecs, q, k, vt, subg)


def _post_kernel(ya_ref, yb_ref, x_ref, wo_ref, gpm_ref, gpf_ref, wgu_ref, wd_ref,
                 gpo_ref, o_ref, act_sc):
    tm = x_ref.shape[0]
    sub_tiles = [slice(r, r + POST_SUB_ROWS) for r in range(0, tm, POST_SUB_ROWS)]
    x1s, hs = [], []
    for rows in sub_tiles:
        mix = (_dot(ya_ref[rows, :], wo_ref[0:GMLP_WIDTH, :])
               + _dot(yb_ref[rows, :], wo_ref[GMLP_WIDTH:GMLP_WIDTH + DIFF_WIDTH, :]))
        x1 = x_ref[rows, :] + _rms(mix, gpm_ref[...])
        x1s.append(x1)
        hs.append(_rms(x1, gpf_ref[...]).astype(_BF16))

    for rows, x1, h in zip(sub_tiles, x1s, hs):
        for lo in range(0, FFN_HIDDEN, FFN_CHUNK):
            gate = _dot(h, wgu_ref[:, lo:lo + FFN_CHUNK])
            up = _dot(h, wgu_ref[:, FFN_HIDDEN + lo:FFN_HIDDEN + lo + FFN_CHUNK])
            act_sc[rows, lo:lo + FFN_CHUNK] = (gate * jax.nn.sigmoid(gate) * up).astype(_BF16)
        y = _dot(act_sc[rows, :], wd_ref[...])
        o_ref[rows, :] = x1 + _rms(y, gpo_ref[...])


def _post(layer, ya, yb, x, wo, gpm, gpf, wgu, wd, gpo):
    n, d = x.shape
    tm = ROW_TILE
    row = lambda i: (i, 0)
    par = functools.partial(_layer_spec, layer=layer)
    once = dict(pipeline_mode=pl.Buffered(1))
    return pl.pallas_call(
        _post_kernel,
        grid=(n // tm,),
        in_specs=[
            pl.BlockSpec((tm, GMLP_WIDTH), row),
            pl.BlockSpec((tm, DIFF_WIDTH), row),
            pl.BlockSpec((tm, d), row),
            par((GMLP_WIDTH + DIFF_WIDTH, d), **once),
            par((1, d)),
            par((1, d)),
            par((d, 2 * FFN_HIDDEN), **once),
            par((FFN_HIDDEN, d), **once),
            par((1, d)),
        ],
        out_specs=pl.BlockSpec((tm, d), row),
        out_shape=jax.ShapeDtypeStruct((n, d), _F32),
        scratch_shapes=[pltpu.VMEM((tm, FFN_HIDDEN), _BF16)],
        compiler_params=pltpu.CompilerParams(
            dimension_semantics=("arbitrary",), vmem_limit_bytes=VMEM_LIMIT_BYTES),
        name="post",
    )(ya, yb, x, wo, gpm, gpf, wgu, wd, gpo)


def _rope_tables(seq):
    half = DIFF_QK_DIM // 2
    inv = 1.0 / (ROPE_THETA ** (jnp.arange(0, DIFF_QK_DIM, 2, dtype=_F32) / DIFF_QK_DIM))
    ang = jnp.arange(seq, dtype=_F32)[:, None] * inv[None, :]
    cos, sin = jnp.cos(ang), jnp.sin(ang)
    zeros = jnp.zeros_like(sin)
    reps = LANES // DIFF_QK_DIM
    cos_t = jnp.tile(jnp.concatenate([cos, cos], axis=1), (1, reps))
    sa_t = jnp.tile(jnp.concatenate([-sin, zeros], axis=1), (1, reps))
    sb_t = jnp.tile(jnp.concatenate([zeros, sin], axis=1), (1, reps))
    del half
    return cos_t, sa_t, sb_t


def kernel(x, pre_mix_g, w_in, gmlp_ln_g, gmlp_ln_b, gmlp_ws, gmlp_b, lambda_q1, lambda_k1,
           lambda_q2, lambda_k2, subln_g, w_out, post_mix_g, pre_ffn_g, w_gate_up, w_down,
           post_ffn_g):
    batch, seq, d = x.shape
    depth = w_in.shape[0]
    assert d == D_MODEL and seq % ROW_TILE == 0 and seq % ATTN_TQ == 0
    assert FFN_HIDDEN % FFN_CHUNK == 0 and ATTN_TQ == 2 * ROW_TILE
    cos_t, sa_t, sb_t = _rope_tables(seq)
    xf = x.reshape(batch * seq, d)
    rows = lambda a: a[:, None, :]
    bias = jnp.broadcast_to(gmlp_b[..., None], gmlp_b.shape + (GMLP_GROUP_DIM,))
    lam_vecs = jnp.stack([lambda_q1, lambda_k1, lambda_q2, lambda_k2], axis=1)
    w_in_b, w_out_b = w_in.astype(_BF16), w_out.astype(_BF16)
    w_gu_b, w_down_b = w_gate_up.astype(_BF16), w_down.astype(_BF16)
    for l in range(depth):
        lam_init = 0.8 - 0.6 * math.exp(-0.3 * l)
        ya, q, k, vt = _inproj(l, xf, rows(pre_mix_g), w_in_b, rows(gmlp_ln_g), rows(gmlp_ln_b),
                               gmlp_ws, bias, cos_t, sa_t, sb_t, seq)
        yb = _attn(l, lam_vecs, q, k, vt, rows(subln_g), batch, seq, lam_init)
        xf = _post(l, ya, yb, xf, w_out_b, rows(post_mix_g), rows(pre_ffn_g), w_gu_b, w_down_b,
                   rows(post_ffn_g))
    return xf.reshape(batch, seq, d)
```

```python
import functools
import math

import jax
import jax.numpy as jnp
from jax import lax
from jax.experimental import pallas as pl
from jax.experimental.pallas import tpu as pltpu

D_MODEL = 1024
CHUNK = 64
GMLP_WIDTH = 512
GMLP_GROUPS = 4
GMLP_GROUP_DIM = 128
GMLP_BLOCK = 128
DIFF_WIDTH = 512
DIFF_HEADS = 4
DIFF_VDIM = 128
DIFF_QK_DIM = 64
ROPE_THETA = 10000.0
FFN_HIDDEN = 2816
IN_WIDTH = 2560
NORM_EPS = 1e-6
NEG_INF = -1e30
ONE_PASS_MAX_STATE = 2.0 ** 100
QK_LOOKAHEAD = 3

LANES = 128
MXU_COLS = 256
BF16_SUBLANES = 16
VT_ROWS = DIFF_VDIM + BF16_SUBLANES
VMEM_LIMIT_BYTES = 56 * 1024 * 1024

ROW_TILE = 512
ATTN_TQ = 2048
ATTN_TK = 512
FFN_CHUNK = 256
POST_SUB_ROWS = 256
INPROJ_SUB_ROWS = 256

_BF16 = jnp.bfloat16
_F32 = jnp.float32


def _dot(a, b):
    return jnp.dot(a, b, preferred_element_type=_F32)


def _rms(x, g):
    return x * lax.rsqrt(jnp.mean(x * x, axis=-1, keepdims=True) + NORM_EPS) * g


def _gelu(x):
    return 0.5 * x * (1.0 + lax.erf(x * math.sqrt(0.5)))


def _inproj_kernel(x_ref, g_ref, w_ref, lng_ref, lnb_ref, ws_ref, bias_ref,
                   cos_ref, sa_ref, sb_ref, ya_ref, q_ref, k_ref, vt_ref):
    tm = x_ref.shape[0]
    o_q = 2 * GMLP_WIDTH
    o_k = o_q + DIFF_WIDTH
    o_v = o_k + DIFF_WIDTH

    def rope(z, rows, scale):
        cos, sa, sb = cos_ref[rows, :], sa_ref[rows, :], sb_ref[rows, :]
        out = []
        for hb in range(z.shape[1] // LANES):
            blk = z[:, hb * LANES:(hb + 1) * LANES]
            rot = (blk * cos + pltpu.roll(blk, LANES - DIFF_QK_DIM // 2, 1) * sa
                   + pltpu.roll(blk, DIFF_QK_DIM // 2, 1) * sb)
            out.append((rot * scale).astype(_BF16))
        return out

    gated = []
    for r0 in range(0, tm, INPROJ_SUB_ROWS):
        rows = slice(r0, r0 + INPROJ_SUB_ROWS)
        h = _rms(x_ref[rows, :], g_ref[...]).astype(_BF16)

        vg = _gelu(_dot(h, w_ref[:, GMLP_WIDTH:2 * GMLP_WIDTH]))
        mu = jnp.mean(vg, axis=-1, keepdims=True)
        vc = vg - mu
        var = jnp.mean(vc * vc, axis=-1, keepdims=True)
        vln = (vc * lax.rsqrt(var + NORM_EPS) * lng_ref[...] + lnb_ref[...]).astype(_BF16)
        u = _gelu(_dot(h, w_ref[:, 0:GMLP_WIDTH]))
        gated.append((r0, u, vln))

        q_scale = DIFF_QK_DIM ** -0.5 * math.log2(math.e)
        for hb, blk in enumerate(rope(_dot(h, w_ref[:, o_q:o_k]), rows, q_scale)):
            q_ref[rows, hb * LANES:(hb + 1) * LANES] = blk
        for hb, blk in enumerate(rope(_dot(h, w_ref[:, o_k:o_v]), rows, 1.0)):
            k_ref[rows, hb * LANES:(hb + 1) * LANES] = blk
        vals = _dot(h, w_ref[:, o_v:IN_WIDTH])
        for hb in range(DIFF_HEADS):
            vt_ref[0, hb, 0, 0:DIFF_VDIM, rows] = (
                vals[:, hb * DIFF_VDIM:(hb + 1) * DIFF_VDIM].T.astype(_BF16))
            vt_ref[0, hb, 0, DIFF_VDIM:VT_ROWS, rows] = jnp.ones(
                (VT_ROWS - DIFF_VDIM, INPROJ_SUB_ROWS), _BF16)

    t_chunk = lax.broadcasted_iota(jnp.int32, (GMLP_BLOCK, GMLP_BLOCK), 0) // CHUNK
    s_chunk = lax.broadcasted_iota(jnp.int32, (GMLP_BLOCK, GMLP_BLOCK), 1) // CHUNK
    causal = s_chunk <= t_chunk
    w_masked = [jnp.where(causal, ws_ref[g], 0.0).astype(_BF16) for g in range(GMLP_GROUPS)]
    for r0, u, vln in gated:
        for g in range(GMLP_GROUPS):
            cols = slice(g * GMLP_GROUP_DIM, (g + 1) * GMLP_GROUP_DIM)
            for r in range(0, INPROJ_SUB_ROWS, GMLP_BLOCK):
                blk = slice(r, r + GMLP_BLOCK)
                mixed = _dot(w_masked[g], vln[blk, cols]) + bias_ref[g]
                ya_ref[r0 + r:r0 + r + GMLP_BLOCK, cols] = (u[blk, cols] * mixed).astype(_BF16)


def _layer_spec(shape, layer, **kwargs):
    index = (layer,) + (0,) * len(shape)
    return pl.BlockSpec((None,) + tuple(shape), lambda *_: index, **kwargs)


def _inproj(layer, x, g, w, lng, lnb, ws, bias, cos, sa, sb, seq):
    n, d = x.shape
    tm = ROW_TILE
    seq_tiles = seq // tm
    row = lambda i: (i, 0)
    par = functools.partial(_layer_spec, layer=layer)
    pos = lambda i: (i % seq_tiles, 0)
    out_sds = jax.ShapeDtypeStruct((n, DIFF_WIDTH), _BF16)
    vt_sds = jax.ShapeDtypeStruct((n // seq, DIFF_HEADS, seq_tiles, VT_ROWS, tm), _BF16)
    vt_spec = pl.BlockSpec((1, DIFF_HEADS, 1, VT_ROWS, tm),
                           lambda i: (i // seq_tiles, 0, i % seq_tiles, 0, 0))
    return pl.pallas_call(
        _inproj_kernel,
        grid=(n // tm,),
        in_specs=[
            pl.BlockSpec((tm, d), row),
            par((1, d)),
            par((d, IN_WIDTH)),
            par((1, GMLP_WIDTH)),
            par((1, GMLP_WIDTH)),
            par((GMLP_GROUPS, GMLP_BLOCK, GMLP_BLOCK)),
            par((GMLP_GROUPS, GMLP_BLOCK, GMLP_GROUP_DIM)),
            pl.BlockSpec((tm, LANES), pos),
            pl.BlockSpec((tm, LANES), pos),
            pl.BlockSpec((tm, LANES), pos),
        ],
        out_specs=[pl.BlockSpec((tm, DIFF_WIDTH), row)] * 3 + [vt_spec],
        out_shape=[out_sds] * 3 + [vt_sds],
        compiler_params=pltpu.CompilerParams(
            dimension_semantics=("arbitrary",), vmem_limit_bytes=VMEM_LIMIT_BYTES),
        name="inproj",
    )(x, g, w, lng, lnb, ws, bias, cos, sa, sb)


def _attn_kernel(lam_ref, q_ref, k_ref, vt_ref, subg_ref, o_ref,
                 qqt_sc, s0_sc, mt0_sc, m_sc, l_sc, acc_sc, *, lam_init):
    t = q_ref.shape[0]
    nblk, tk, cw = s0_sc.shape
    i = pl.program_id(2)
    s_sc, mt_sc = (s0_sc,), (mt0_sc,)
    lanes = lambda c: slice(c * cw, (c + 1) * cw)

    qt = q_ref[...].astype(_F32).T
    sub = lax.broadcasted_iota(jnp.int32, (LANES, cw), 0)
    for c in range(nblk):
        branch = sub < DIFF_QK_DIM if c < nblk // 2 else sub >= DIFF_QK_DIM
        qqt_sc[c] = jnp.where(branch, qt[:, lanes(c % (nblk // 2))], 0.0).astype(_BF16)

    def visible_keys(c, diag):
        if diag is None:
            return tk
        return min(max(c % (nblk // 2) * cw + cw - diag * tk, 0), tk)

    def qk(j, b, diag=None):
        start = pl.multiple_of(j * tk, tk)
        for c in range(nblk):
            nk = visible_keys(c, diag)
            if nk == 0:
                continue
            st = _dot(k_ref[pl.ds(start, nk), :], qqt_sc[c])
            if diag is not None:
                kc = (lax.broadcasted_iota(jnp.int32, st.shape, 0) + diag * tk) // CHUNK
                qc = (lax.broadcasted_iota(jnp.int32, st.shape, 1) + c % (nblk // 2) * cw) // CHUNK
                st = jnp.where(kc <= qc, st, NEG_INF)
            s_sc[b][c, 0:nk, :] = st
            mt_sc[b][:, lanes(c)] = jnp.max(st, axis=0, keepdims=True)

    def softmax_pv(j, b, diag=None, first=False):
        for c in range(nblk):
            nk = visible_keys(c, diag)
            if nk == 0:
                assert not first
                continue
            if first:
                m_new = mt_sc[b][:, lanes(c)]
            else:
                m_prev = m_sc[:, lanes(c)]
                m_new = jnp.maximum(m_prev, mt_sc[b][:, lanes(c)])
                alpha = jnp.exp2(m_prev - m_new)
            p = jnp.exp2(s_sc[b][c, 0:nk, :] - m_new)
            m_sc[:, lanes(c)] = m_new
            upd = _dot(vt_ref[0, 0, j, :, 0:nk], p.astype(_BF16))
            if first:
                l_sc[:, lanes(c)] = upd[DIFF_VDIM:DIFF_VDIM + 1]
                acc_sc[c] = upd[0:DIFF_VDIM]
            else:
                l_sc[:, lanes(c)] = alpha * l_sc[:, lanes(c)] + upd[DIFF_VDIM:DIFF_VDIM + 1]
                acc_sc[c] = alpha * acc_sc[c] + upd[0:DIFF_VDIM]

    def stream(*tiles):
        def scores(j, diag, c):
            nk = visible_keys(c, diag)
            start = pl.multiple_of(j * tk, tk)
            st = _dot(k_ref[pl.ds(start, nk), :], qqt_sc[c])
            if diag is not None:
                kc = (lax.broadcasted_iota(jnp.int32, st.shape, 0) + diag * tk) // CHUNK
                qc = (lax.broadcasted_iota(jnp.int32, st.shape, 1) + c % (nblk // 2) * cw) // CHUNK
                st = jnp.where(kc <= qc, st, NEG_INF)
            return st

        items = [(j, diag, c) for j, diag in tiles for c in range(nblk)
                 if visible_keys(c, diag) > 0]
        ahead = [scores(*item) for item in items[:QK_LOOKAHEAD]]
        for n, (j, diag, c) in enumerate(items):
            nk = visible_keys(c, diag)
            st = ahead.pop(0)
            if n + QK_LOOKAHEAD < len(items):
                ahead.append(scores(*items[n + QK_LOOKAHEAD]))
            p = jnp.exp2(st - m_sc[:, lanes(c)])
            upd = _dot(vt_ref[0, 0, j, :, 0:nk], p.astype(_BF16))
            l_sc[:, lanes(c)] = l_sc[:, lanes(c)] + upd[DIFF_VDIM:DIFF_VDIM + 1]
            acc_sc[c] = acc_sc[c] + upd[0:DIFF_VDIM]

    n_diag = t // tk
    n_full = n_diag * i

    def two_pass_sweep():
        @pl.when(i > 0)
        def _():
            qk(0, 0)
            softmax_pv(0, 0, first=True)

            def one(j, carry):
                qk(j, 0)
                softmax_pv(j, 0)
                return carry

            lax.fori_loop(1, n_full, one, 0)
            for d in range(n_diag):
                qk(n_full + d, 0, diag=d)
                softmax_pv(n_full + d, 0, diag=d)

        @pl.when(i == 0)
        def _():
            for d in range(n_diag):
                qk(d, 0, diag=d)
                softmax_pv(d, 0, diag=d, first=(d == 0))

    @pl.when(i > 0)
    def _():
        qk(0, 0)
        softmax_pv(0, 0, first=True)

        def pair(u, carry):
            j = 2 * u + 1
            stream((j, None), (j + 1, None))
            return carry

        lax.fori_loop(0, (n_full - 1) // 2, pair, 0)
        stream((n_full - 1, None), *[(n_full + d, d) for d in range(n_diag)])

    @pl.when(i == 0)
    def _():
        qk(0, 0, diag=0)
        softmax_pv(0, 0, diag=0, first=True)
        stream(*[(d, d) for d in range(1, n_diag)])

    def finalize():
        lam_v = lam_ref[...]
        lam = (jnp.exp(jnp.sum(lam_v[0:1] * lam_v[1:2], axis=-1, keepdims=True))
               - jnp.exp(jnp.sum(lam_v[2:3] * lam_v[3:4], axis=-1, keepdims=True)) + lam_init)
        unsafe = jnp.sum(jnp.where(l_sc[...] < ONE_PASS_MAX_STATE, 0.0, 1.0))
        half = nblk // 2
        for c in range(half):
            ot = (acc_sc[c] / l_sc[:, lanes(c)]
                  - lam * (acc_sc[c + half] / l_sc[:, lanes(c + half)]))
            ms = jnp.mean(ot * ot, axis=0, keepdims=True)
            unsafe = unsafe + jnp.sum(jnp.where(ms < ONE_PASS_MAX_STATE, 0.0, 1.0))
            ot = ot * lax.rsqrt(ms + NORM_EPS)
            o_ref[lanes(c), :] = (ot.T * subg_ref[...] * (1.0 - lam_init)).astype(_BF16)
        return unsafe

    @pl.when(finalize() > 0.0)
    def _():
        two_pass_sweep()
        finalize()


def _attn(layer, lam_vecs, q, k, vt, subg, batch, seq, lam_init):
    t = ATTN_TQ
    tk = ATTN_TK
    nq = seq // t
    assert vt.shape[2:] == (seq // tk, VT_ROWS, tk)
    nblk = 2 * t // MXU_COLS
    stat = pltpu.VMEM((1, 2 * t), _F32)
    return pl.pallas_call(
        functools.partial(_attn_kernel, lam_init=lam_init),
        grid=(batch, DIFF_HEADS, nq),
        in_specs=[
            _layer_spec((4, DIFF_QK_DIM), layer),
            pl.BlockSpec((t, LANES), lambda b, h, i: (b * nq + i, h)),
            pl.BlockSpec((seq, LANES), lambda b, h, i: (b, h)),
            pl.BlockSpec((1, 1, seq // tk, VT_ROWS, tk), lambda b, h, i: (b, h, 0, 0, 0)),
            _layer_spec((1, DIFF_VDIM), layer),
        ],
        out_specs=pl.BlockSpec((t, LANES), lambda b, h, i: (b * nq + i, h)),
        out_shape=jax.ShapeDtypeStruct(q.shape, _BF16),
        scratch_shapes=[
            pltpu.VMEM((nblk, LANES, MXU_COLS), _BF16),
            pltpu.VMEM((nblk, tk, MXU_COLS), _F32),
            stat, stat, stat,
            pltpu.VMEM((nblk, DIFF_VDIM, MXU_COLS), _F32),
        ],
        compiler_params=pltpu.CompilerParams(
            dimension_semantics=("arbitrary", "arbitrary", "arbitrary"),
            vmem_limit_bytes=VMEM_LIMIT_BYTES),
        name="diff_attn",
    )(lam_vecs, q, k, vt, subg)


def _post_kernel(ya_ref, yb_ref, x_ref, wo_ref, gpm_ref, gpf_ref, wgu_ref, wd_ref,
                 gpo_ref, o_ref, act_sc):
    tm = x_ref.shape[0]
    sub_tiles = [slice(r, r + POST_SUB_ROWS) for r in range(0, tm, POST_SUB_ROWS)]
    x1s, hs = [], []
    for rows in sub_tiles:
        mix = (_dot(ya_ref[rows, :], wo_ref[0:GMLP_WIDTH, :])
               + _dot(yb_ref[rows, :], wo_ref[GMLP_WIDTH:GMLP_WIDTH + DIFF_WIDTH, :]))
        x1 = x_ref[rows, :] + _rms(mix, gpm_ref[...])
        x1s.append(x1)
        hs.append(_rms(x1, gpf_ref[...]).astype(_BF16))

    for rows, x1, h in zip(sub_tiles, x1s, hs):
        for lo in range(0, FFN_HIDDEN, FFN_CHUNK):
            gate = _dot(h, wgu_ref[:, lo:lo + FFN_CHUNK])
            up = _dot(h, wgu_ref[:, FFN_HIDDEN + lo:FFN_HIDDEN + lo + FFN_CHUNK])
            act_sc[rows, lo:lo + FFN_CHUNK] = (gate * jax.nn.sigmoid(gate) * up).astype(_BF16)
        y = _dot(act_sc[rows, :], wd_ref[...])
        o_ref[rows, :] = x1 + _rms(y, gpo_ref[...])


def _post(layer, ya, yb, x, wo, gpm, gpf, wgu, wd, gpo):
    n, d = x.shape
    tm = ROW_TILE
    row = lambda i: (i, 0)
    par = functools.partial(_layer_spec, layer=layer)
    once = dict(pipeline_mode=pl.Buffered(1))
    return pl.pallas_call(
        _post_kernel,
        grid=(n // tm,),
        in_specs=[
            pl.BlockSpec((tm, GMLP_WIDTH), row),
            pl.BlockSpec((tm, DIFF_WIDTH), row),
            pl.BlockSpec((tm, d), row),
            par((GMLP_WIDTH + DIFF_WIDTH, d), **once),
            par((1, d)),
            par((1, d)),
            par((d, 2 * FFN_HIDDEN), **once),
            par((FFN_HIDDEN, d), **once),
            par((1, d)),
        ],
        out_specs=pl.BlockSpec((tm, d), row),
        out_shape=jax.ShapeDtypeStruct((n, d), _F32),
        scratch_shapes=[pltpu.VMEM((tm, FFN_HIDDEN), _BF16)],
        compiler_params=pltpu.CompilerParams(
            dimension_semantics=("arbitrary",), vmem_limit_bytes=VMEM_LIMIT_BYTES),
        name="post",
    )(ya, yb, x, wo, gpm, gpf, wgu, wd, gpo)


def _rope_tables(seq):
    half = DIFF_QK_DIM // 2
    inv = 1.0 / (ROPE_THETA ** (jnp.arange(0, DIFF_QK_DIM, 2, dtype=_F32) / DIFF_QK_DIM))
    ang = jnp.arange(seq, dtype=_F32)[:, None] * inv[None, :]
    cos, sin = jnp.cos(ang), jnp.sin(ang)
    zeros = jnp.zeros_like(sin)
    reps = LANES // DIFF_QK_DIM
    cos_t = jnp.tile(jnp.concatenate([cos, cos], axis=1), (1, reps))
    sa_t = jnp.tile(jnp.concatenate([-sin, zeros], axis=1), (1, reps))
    sb_t = jnp.tile(jnp.concatenate([zeros, sin], axis=1), (1, reps))
    del half
    return cos_t, sa_t, sb_t


def kernel(x, pre_mix_g, w_in, gmlp_ln_g, gmlp_ln_b, gmlp_ws, gmlp_b, lambda_q1, lambda_k1,
           lambda_q2, lambda_k2, subln_g, w_out, post_mix_g, pre_ffn_g, w_gate_up, w_down,
           post_ffn_g):
    batch, seq, d = x.shape
    depth = w_in.shape[0]
    assert d == D_MODEL and seq % ROW_TILE == 0 and seq % ATTN_TQ == 0
    assert FFN_HIDDEN % FFN_CHUNK == 0 and ATTN_TK == ROW_TILE and ATTN_TQ % (2 * ATTN_TK) == 0
    cos_t, sa_t, sb_t = _rope_tables(seq)
    xf = x.reshape(batch * seq, d)
    rows = lambda a: a[:, None, :]
    bias = jnp.broadcast_to(gmlp_b[..., None], gmlp_b.shape + (GMLP_GROUP_DIM,))
    lam_vecs = jnp.stack([lambda_q1, lambda_k1, lambda_q2, lambda_k2], axis=1)
    w_in_b, w_out_b = w_in.astype(_BF16), w_out.astype(_BF16)
    w_gu_b, w_down_b = w_gate_up.astype(_BF16), w_down.astype(_BF16)
    for l in range(depth):
        lam_init = 0.8 - 0.6 * math.exp(-0.3 * l)
        ya, q, k, vt = _inproj(l, xf, rows(pre_mix_g), w_in_b, rows(gmlp_ln_g), rows(gmlp_ln_b),
                               gmlp_ws, bias, cos_t, sa_t, sb_t, seq)
        yb = _attn(l, lam_vecs, q, k, vt, rows(subln_g), batch, seq, lam_init)
        xf = _post(l, ya, yb, xf, w_out_b, rows(post_mix_g), rows(pre_ffn_g), w_gu_b, w_down_b,
                   rows(post_ffn_g))
    return xf.reshape(batch, seq, d)
```
